```python
import math
import jax
import jax.numpy as jnp
from jax import lax
import numpy as np

D_MODEL = 1024
BATCH = 8
SEQ = 8192
DEPTH = 2
DEC_BATCH = 16
DEC_SEQ = 16
PAST_LEN = 1024

CHUNK = 64
N_EVEN = (DEPTH + 1) // 2
N_ODD = DEPTH // 2
RMS_EPS = 1e-6
L2_EPS = 1e-6

S5_WIDTH = D_MODEL // 2
S5_GROUP = 16
S5_GROUPS = S5_WIDTH // S5_GROUP
S5_P = 64

GDN_DK = 128
GDN_DV = 128
GDN_HEADS = (D_MODEL - S5_WIDTH) // GDN_DV
GDN_QK = GDN_HEADS * GDN_DK
GDN_V = GDN_HEADS * GDN_DV
GDN_CONV = 4
GDN_CONV_CH = 2 * GDN_QK + GDN_V

OFF_QKV = S5_WIDTH
OFF_Z = OFF_QKV + GDN_CONV_CH
OFF_B = OFF_Z + GDN_V
OFF_A = OFF_B + GDN_HEADS
IN_COLS = OFF_A + GDN_HEADS
D_MIX_AB = S5_WIDTH + GDN_V

SWA_HEADS = 16
SWA_KV_HEADS = 4
SWA_GROUPS = SWA_HEADS // SWA_KV_HEADS
SWA_HEAD_DIM = 64
WINDOW = 128
WIN_CHUNKS = -(-WINDOW // CHUNK)
KV_WIN = min(WINDOW, PAST_LEN)

D_FF = -(-8 * D_MODEL // (3 * 256)) * 256

kernel_name = "hybrid_s5_gdn_swa_stream_step"


def rmsnorm(x, w):
    xf = x.astype(jnp.float32)
    y = xf * lax.rsqrt(jnp.mean(xf * xf, axis=-1, keepdims=True) + RMS_EPS)
    return (y * w.astype(jnp.float32)).astype(x.dtype)


def l2norm(x):
    return x * lax.rsqrt(jnp.sum(x * x, axis=-1, keepdims=True) + L2_EPS)


def swiglu(h, w_gate, w_up, w_down):
    return (jax.nn.silu(h @ w_gate) * (h @ w_up)) @ w_down


def s5_mixer(u, x0_re, x0_im, lam_re, lam_im, log_dt, b_re, b_im, c_re, c_im, d, w_glu, b_glu, chunk):
    f32 = jnp.float32
    bsz, L, _ = u.shape
    n = L // chunk
    uf = u.astype(f32).reshape(bsz, L, S5_GROUPS, S5_GROUP)
    lam = lax.complex(lam_re.astype(f32), lam_im.astype(f32))
    dt = jnp.exp(log_dt.astype(f32))[:, None]
    lam_bar = jnp.exp(lam * dt)
    b_bar = ((lam_bar - 1.0) / lam)[..., None] * lax.complex(b_re.astype(f32), b_im.astype(f32))
    c = lax.complex(c_re.astype(f32), c_im.astype(f32))
    a_blk = jnp.broadcast_to(lam_bar, (bsz, chunk, S5_GROUPS, S5_P))

    def combine(e1, e2):
        a1, b1 = e1
        a2, b2 = e2
        return a1 * a2, a2 * b1 + b2

    def step(carry, u_blk):
        bu = jnp.einsum('gpc,btgc->btgp', b_bar, u_blk.astype(jnp.complex64))
        bu = bu.at[:, 0].add(lam_bar * carry)
        _, xs = lax.associative_scan(combine, (a_blk, bu), axis=1)
        y_blk = jnp.einsum('gcp,btgp->btgc', c, xs).real
        return xs[:, -1], y_blk

    x0 = lax.complex(x0_re.astype(f32), x0_im.astype(f32))
    u_blocks = jnp.moveaxis(uf.reshape(bsz, n, chunk, S5_GROUPS, S5_GROUP), 1, 0)
    x_last, y = lax.scan(step, x0, u_blocks)
    y = jnp.moveaxis(y, 0, 1).reshape(bsz, L, S5_GROUPS, S5_GROUP)
    y = y + d.astype(f32).reshape(S5_GROUPS, S5_GROUP) * uf
    y = y.reshape(bsz, L, S5_WIDTH).astype(u.dtype)
    z = jax.nn.gelu(y)
    out = z * jax.nn.sigmoid(z @ w_glu + b_glu)
    return out, x_last.real, x_last.imag


def causal_conv(x, buf, w):
    L = x.shape[1]
    xp = jnp.concatenate([buf.astype(x.dtype), x], axis=1)
    y = xp[:, 0:L] * w[0]
    for j in range(1, GDN_CONV):
        y = y + xp[:, j:j + L] * w[j]
    return y, xp[:, -(GDN_CONV - 1):]


def gated_delta_chunked(q, k, v, g, beta, s0, chunk):
    bsz, L, H, _ = q.shape
    n = L // chunk

    def blk(t):
        t = t.reshape((bsz, n, chunk, H) + t.shape[3:])
        return jnp.moveaxis(t, 3, 1)

    q, k, v, g, beta = blk(q), blk(k), blk(v), blk(g), blk(beta)
    gc = jnp.cumsum(g, axis=-1)
    idx = jnp.arange(chunk)
    incl = idx[:, None] >= idx[None, :]
    strict = idx[:, None] > idx[None, :]
    decay = jnp.exp(jnp.where(incl, gc[..., :, None] - gc[..., None, :], -jnp.inf))
    kb = k * beta[..., None]
    a_mat = jnp.where(strict, jnp.einsum('bhnid,bhnjd->bhnij', kb, k) * decay, 0.0)
    eye = jnp.eye(chunk, dtype=q.dtype)
    t_inv = lax.linalg.triangular_solve(a_mat + eye, jnp.broadcast_to(eye, a_mat.shape),
                                        left_side=True, lower=True, unit_diagonal=True)
    gexp = jnp.exp(gc)[..., None]
    u_val = t_inv @ (v * beta[..., None])
    w_key = t_inv @ (kb * gexp)
    q_dec = q * gexp
    attn = jnp.einsum('bhnid,bhnjd->bhnij', q, k) * decay
    k_dec = k * jnp.exp(gc[..., -1:] - gc)[..., None]
    g_last = jnp.exp(gc[..., -1])

    def step(s, xs):
        u_c, w_c, q_c, a_c, k_c, gl = xs
        v_new = u_c - w_c @ s
        o = q_c @ s + a_c @ v_new
        s = s * gl[..., None, None] + jnp.swapaxes(k_c, -1, -2) @ v_new
        return s, o

    xs = tuple(jnp.moveaxis(t, 2, 0) for t in (u_val, w_key, q_dec, attn, k_dec, g_last))
    s_fin, o = lax.scan(step, s0, xs)
    o = jnp.moveaxis(jnp.moveaxis(o, 0, 2), 1, 3).reshape(bsz, L, H, -1)
    return o, s_fin


def gdn_mixer(qkv, z, b, a, conv_buf, s0, conv_w, a_log, dt_bias, norm_w, chunk):
    f32 = jnp.float32
    bsz, L, _ = qkv.shape
    y, new_buf = causal_conv(qkv, conv_buf, conv_w)
    y = jax.nn.silu(y).astype(f32)
    q = l2norm(y[..., :GDN_QK].reshape(bsz, L, GDN_HEADS, GDN_DK)) * (GDN_DK ** -0.5)
    k = l2norm(y[..., GDN_QK:2 * GDN_QK].reshape(bsz, L, GDN_HEADS, GDN_DK))
    v = y[..., 2 * GDN_QK:].reshape(bsz, L, GDN_HEADS, GDN_DV)
    beta = jax.nn.sigmoid(b.astype(f32))
    g = -jnp.exp(a_log.astype(f32)) * jax.nn.softplus(a.astype(f32) + dt_bias.astype(f32))
    o, s_fin = gated_delta_chunked(q, k, v, g, beta, s0.astype(f32), chunk)
    o = rmsnorm(o, norm_w) * jax.nn.silu(z.astype(f32).reshape(bsz, L, GDN_HEADS, GDN_DV))
    return o.reshape(bsz, L, GDN_V).astype(qkv.dtype), s_fin, new_buf


def mixer_ab(h, x0_re, x0_im, s0, conv_buf, w_in, lam_re, lam_im, log_dt, b_re, b_im, c_re, c_im,
             d, w_glu, b_glu, conv_w, a_log, dt_bias, norm_w, w_out, chunk):
    proj = h @ w_in
    a_out, x_re, x_im = s5_mixer(proj[..., :OFF_QKV], x0_re, x0_im, lam_re, lam_im, log_dt,
                                 b_re, b_im, c_re, c_im, d, w_glu, b_glu, chunk)
    b_out, s_fin, new_buf = gdn_mixer(proj[..., OFF_QKV:OFF_Z], proj[..., OFF_Z:OFF_B],
                                      proj[..., OFF_B:OFF_A], proj[..., OFF_A:IN_COLS],
                                      conv_buf, s0, conv_w, a_log, dt_bias, norm_w, chunk)
    out = jnp.concatenate([a_out, b_out], axis=-1) @ w_out
    return out, x_re, x_im, s_fin, new_buf


def sink_softmax(scores, sinks):
    sk = jnp.broadcast_to(sinks.astype(jnp.float32).reshape(SWA_KV_HEADS, SWA_GROUPS, 1, 1),
                          scores.shape[:-1] + (1,))
    return jax.nn.softmax(jnp.concatenate([scores, sk], axis=-1), axis=-1)[..., :-1]


def swa_prompt(h, wq, wk, wv, sinks, wo):
    bsz, L, _ = h.shape
    n = L // CHUNK
    span = (WIN_CHUNKS + 1) * CHUNK
    q = (h @ wq).reshape(bsz, n, CHUNK, SWA_KV_HEADS, SWA_GROUPS, SWA_HEAD_DIM)
    k = (h @ wk).reshape(bsz, L, SWA_KV_HEADS, SWA_HEAD_DIM)
    v = (h @ wv).reshape(bsz, L, SWA_KV_HEADS, SWA_HEAD_DIM)
    pad = ((0, 0), (WIN_CHUNKS, 0), (0, 0), (0, 0), (0, 0))
    kp = jnp.pad(k.reshape(bsz, n, CHUNK, SWA_KV_HEADS, SWA_HEAD_DIM), pad)
    vp = jnp.pad(v.reshape(bsz, n, CHUNK, SWA_KV_HEADS, SWA_HEAD_DIM), pad)
    kband = jnp.concatenate([kp[:, j:j + n] for j in range(WIN_CHUNKS + 1)], axis=2)
    vband = jnp.concatenate([vp[:, j:j + n] for j in range(WIN_CHUNKS + 1)], axis=2)
    key_chunk = jnp.arange(n)[:, None] - WIN_CHUNKS + jnp.arange(span)[None, :] // CHUNK
    valid = key_chunk >= 0
    scores = jnp.einsum('bnqkgd,bnskd->bnkgqs', q, kband).astype(jnp.float32) * (SWA_HEAD_DIM ** -0.5)
    scores = jnp.where(valid[None, :, None, None, None, :], scores, -1e30)
    p = sink_softmax(scores, sinks).astype(vband.dtype)
    o = jnp.einsum('bnkgqs,bnskd->bnqkgd', p, vband).reshape(bsz, L, SWA_HEADS * SWA_HEAD_DIM)
    return o @ wo, k[:, -KV_WIN:], v[:, -KV_WIN:]


def swa_sample(h, cache_k, cache_v, wq, wk, wv, sinks, wo):
    bsz, L, _ = h.shape
    q = (h @ wq).reshape(bsz, L, SWA_KV_HEADS, SWA_GROUPS, SWA_HEAD_DIM)
    k_new = (h @ wk).reshape(bsz, L, SWA_KV_HEADS, SWA_HEAD_DIM)
    v_new = (h @ wv).reshape(bsz, L, SWA_KV_HEADS, SWA_HEAD_DIM)
    k_all = jnp.concatenate([cache_k.astype(k_new.dtype), k_new], axis=1)
    v_all = jnp.concatenate([cache_v.astype(v_new.dtype), v_new], axis=1)
    scores = jnp.einsum('bqkgd,bskd->bkgqs', q, k_all).astype(jnp.float32) * (SWA_HEAD_DIM ** -0.5)
    p = sink_softmax(scores, sinks).astype(v_all.dtype)
    o = jnp.einsum('bkgqs,bskd->bqkgd', p, v_all).reshape(bsz, L, SWA_HEADS * SWA_HEAD_DIM)
    return o @ wo, k_all[:, -KV_WIN:], v_all[:, -KV_WIN:]


def trunk(x, s5_re, s5_im, gdn_s, gdn_conv, kv_k, kv_v, p, is_prompt):
    chunk = CHUNK if is_prompt else x.shape[1]
    n_re, n_im, n_s, n_conv, n_k, n_v = [], [], [], [], [], []
    for layer in range(DEPTH):
        j = layer // 2
        h = rmsnorm(x, p['norm_mix'][layer])
        if layer % 2 == 0:
            mix, x_re, x_im, s_fin, buf = mixer_ab(
                h, s5_re[j], s5_im[j], gdn_s[j], gdn_conv[j], p['w_in'][j],
                p['s5_lam_re'][j], p['s5_lam_im'][j], p['s5_log_dt'][j], p['s5_b_re'][j], p['s5_b_im'][j],
                p['s5_c_re'][j], p['s5_c_im'][j], p['s5_d'][j], p['s5_w_glu'][j], p['s5_b_glu'][j],
                p['gdn_conv_w'][j], p['gdn_a_log'][j], p['gdn_dt_bias'][j], p['gdn_norm_w'][j],
                p['w_out_ab'][j], chunk)
            n_re.append(x_re)
            n_im.append(x_im)
            n_s.append(s_fin)
            n_conv.append(buf)
        else:
            if is_prompt:
                mix, ck, cv = swa_prompt(h, p['swa_wq'][j], p['swa_wk'][j], p['swa_wv'][j],
                                         p['swa_sinks'][j], p['swa_wo'][j])
            else:
                mix, ck, cv = swa_sample(h, kv_k[j], kv_v[j], p['swa_wq'][j], p['swa_wk'][j],
                                         p['swa_wv'][j], p['swa_sinks'][j], p['swa_wo'][j])
            n_k.append(ck)
            n_v.append(cv)
        x = x + mix
        x = x + swiglu(rmsnorm(x, p['norm_ffn'][layer]), p['ffn_w_gate'][layer],
                       p['ffn_w_up'][layer], p['ffn_w_down'][layer])
    y = rmsnorm(x, p['norm_final'])
    return y, jnp.stack(n_re), jnp.stack(n_im), jnp.stack(n_s), jnp.stack(n_conv), jnp.stack(n_k), jnp.stack(n_v)


def setup_inputs(seed: int = 0) -> dict:
    key = jax.random.key(seed)
    ks = iter(jax.random.split(key, 48))
    f32 = jnp.float32

    def nrm(shape, scale):
        return scale * jax.random.normal(next(ks), shape, f32)

    def unif(shape, lo, hi):
        return jax.random.uniform(next(ks), shape, f32, lo, hi)

    x_prompt = nrm((BATCH, SEQ, D_MODEL), 1.0)
    x_sample = nrm((DEC_BATCH, DEC_SEQ, D_MODEL), 1.0)
    state_s5_re = nrm((N_EVEN, DEC_BATCH, S5_GROUPS, S5_P), 0.5)
    state_s5_im = nrm((N_EVEN, DEC_BATCH, S5_GROUPS, S5_P), 0.5)
    state_gdn = nrm((N_EVEN, DEC_BATCH, GDN_HEADS, GDN_DK, GDN_DV), 0.1)
    state_gdn_conv = nrm((N_EVEN, DEC_BATCH, GDN_CONV - 1, GDN_CONV_CH), 1.0)
    cache_swa_k = nrm((N_ODD, DEC_BATCH, KV_WIN, SWA_KV_HEADS, SWA_HEAD_DIM), 1.0)
    cache_swa_v = nrm((N_ODD, DEC_BATCH, KV_WIN, SWA_KV_HEADS, SWA_HEAD_DIM), 1.0)
    norm_mix = 1.0 + nrm((DEPTH, D_MODEL), 0.02)
    norm_ffn = 1.0 + nrm((DEPTH, D_MODEL), 0.02)
    norm_final = 1.0 + nrm((D_MODEL,), 0.02)
    w_in = nrm((N_EVEN, D_MODEL, IN_COLS), D_MODEL ** -0.5)
    n_idx = jnp.arange(S5_P, dtype=f32)
    s5_lam_re = -0.5 + nrm((N_EVEN, S5_GROUPS, S5_P), 0.01)
    s5_lam_im = math.pi * n_idx + nrm((N_EVEN, S5_GROUPS, S5_P), 0.01)
    s5_log_dt = unif((N_EVEN, S5_GROUPS), math.log(1e-3), math.log(1e-1))
    s5_b_re = nrm((N_EVEN, S5_GROUPS, S5_P, S5_GROUP), (2 * S5_GROUP) ** -0.5)
    s5_b_im = nrm((N_EVEN, S5_GROUPS, S5_P, S5_GROUP), (2 * S5_GROUP) ** -0.5)
    s5_c_re = nrm((N_EVEN, S5_GROUPS, S5_GROUP, S5_P), (2 * S5_P) ** -0.5)
    s5_c_im = nrm((N_EVEN, S5_GROUPS, S5_GROUP, S5_P), (2 * S5_P) ** -0.5)
    s5_d = nrm((N_EVEN, S5_WIDTH), 0.5)
    s5_w_glu = nrm((N_EVEN, S5_WIDTH, S5_WIDTH), S5_WIDTH ** -0.5)
    s5_b_glu = nrm((N_EVEN, S5_WIDTH), 0.01)
    gdn_conv_w = nrm((N_EVEN, GDN_CONV, GDN_CONV_CH), GDN_CONV ** -0.5)
    gdn_a_log = jnp.log(unif((N_EVEN, GDN_HEADS), 1.0, 16.0))
    dt0 = jnp.exp(unif((N_EVEN, GDN_HEADS), math.log(1e-3), math.log(1e-1)))
    gdn_dt_bias = dt0 + jnp.log(-jnp.expm1(-dt0))
    gdn_norm_w = 1.0 + nrm((N_EVEN, GDN_DV), 0.02)
    w_out_ab = nrm((N_EVEN, D_MIX_AB, D_MODEL), D_MIX_AB ** -0.5)
    swa_wq = nrm((N_ODD, D_MODEL, SWA_HEADS * SWA_HEAD_DIM), D_MODEL ** -0.5)
    swa_wk = nrm((N_ODD, D_MODEL, SWA_KV_HEADS * SWA_HEAD_DIM), D_MODEL ** -0.5)
    swa_wv = nrm((N_ODD, D_MODEL, SWA_KV_HEADS * SWA_HEAD_DIM), D_MODEL ** -0.5)
    swa_sinks = nrm((N_ODD, SWA_HEADS), 0.5)
    swa_wo = nrm((N_ODD, SWA_HEADS * SWA_HEAD_DIM, D_MODEL), (SWA_HEADS * SWA_HEAD_DIM) ** -0.5)
    ffn_w_gate = nrm((DEPTH, D_MODEL, D_FF), D_MODEL ** -0.5)
    ffn_w_up = nrm((DEPTH, D_MODEL, D_FF), D_MODEL ** -0.5)
    ffn_w_down = nrm((DEPTH, D_FF, D_MODEL), D_FF ** -0.5)
    return {"x_prompt": x_prompt, "x_sample": x_sample,
            "state_s5_re": state_s5_re, "state_s5_im": state_s5_im, "state_gdn": state_gdn,
            "state_gdn_conv": state_gdn_conv, "cache_swa_k": cache_swa_k, "cache_swa_v": cache_swa_v,
            "norm_mix": norm_mix, "norm_ffn": norm_ffn, "norm_final": norm_final, "w_in": w_in,
            "s5_lam_re": s5_lam_re, "s5_lam_im": s5_lam_im, "s5_log_dt": s5_log_dt,
            "s5_b_re": s5_b_re, "s5_b_im": s5_b_im, "s5_c_re": s5_c_re, "s5_c_im": s5_c_im,
            "s5_d": s5_d, "s5_w_glu": s5_w_glu, "s5_b_glu": s5_b_glu,
            "gdn_conv_w": gdn_conv_w, "gdn_a_log": gdn_a_log, "gdn_dt_bias": gdn_dt_bias,
            "gdn_norm_w": gdn_norm_w, "w_out_ab": w_out_ab,
            "swa_wq": swa_wq, "swa_wk": swa_wk, "swa_wv": swa_wv, "swa_sinks": swa_sinks, "swa_wo": swa_wo,
            "ffn_w_gate": ffn_w_gate, "ffn_w_up": ffn_w_up, "ffn_w_down": ffn_w_down}


def reference(x_prompt, x_sample, state_s5_re, state_s5_im, state_gdn, state_gdn_conv, cache_swa_k, cache_swa_v,
              norm_mix, norm_ffn, norm_final, w_in, s5_lam_re, s5_lam_im, s5_log_dt, s5_b_re, s5_b_im,
              s5_c_re, s5_c_im, s5_d, s5_w_glu, s5_b_glu, gdn_conv_w, gdn_a_log, gdn_dt_bias, gdn_norm_w,
              w_out_ab, swa_wq, swa_wk, swa_wv, swa_sinks, swa_wo, ffn_w_gate, ffn_w_up, ffn_w_down):
    p = dict(norm_mix=norm_mix, norm_ffn=norm_ffn, norm_final=norm_final, w_in=w_in,
             s5_lam_re=s5_lam_re, s5_lam_im=s5_lam_im, s5_log_dt=s5_log_dt, s5_b_re=s5_b_re, s5_b_im=s5_b_im,
             s5_c_re=s5_c_re, s5_c_im=s5_c_im, s5_d=s5_d, s5_w_glu=s5_w_glu, s5_b_glu=s5_b_glu,
             gdn_conv_w=gdn_conv_w, gdn_a_log=gdn_a_log, gdn_dt_bias=gdn_dt_bias, gdn_norm_w=gdn_norm_w,
             w_out_ab=w_out_ab, swa_wq=swa_wq, swa_wk=swa_wk, swa_wv=swa_wv, swa_sinks=swa_sinks,
             swa_wo=swa_wo, ffn_w_gate=ffn_w_gate, ffn_w_up=ffn_w_up, ffn_w_down=ffn_w_down)
    bsz = x_prompt.shape[0]
    f32 = jnp.float32
    z_s5 = jnp.zeros((N_EVEN, bsz, S5_GROUPS, S5_P), f32)
    z_gdn = jnp.zeros((N_EVEN, bsz, GDN_HEADS, GDN_DK, GDN_DV), f32)
    z_conv = jnp.zeros((N_EVEN, bsz, GDN_CONV - 1, GDN_CONV_CH), f32)
    y_prompt, p_s5_re, p_s5_im, p_gdn, p_gdn_conv, p_swa_k, p_swa_v = trunk(
        x_prompt, z_s5, z_s5, z_gdn, z_conv, None, None, p, True)
    y_sample, s_s5_re, s_s5_im, s_gdn, s_gdn_conv, s_swa_k, s_swa_v = trunk(
        x_sample, state_s5_re, state_s5_im, state_gdn, state_gdn_conv, cache_swa_k, cache_swa_v, p, False)
    return (y_prompt, y_sample, p_s5_re, p_s5_im, p_gdn, p_gdn_conv, p_swa_k, p_swa_v,
            s_s5_re, s_s5_im, s_gdn, s_gdn_conv, s_swa_k, s_swa_v)
```

```python
import functools
import math

import jax
import jax.numpy as jnp
from jax import lax
from jax.experimental import pallas as pl
from jax.experimental.pallas import tpu as pltpu

F32 = jnp.float32
BF16 = jnp.bfloat16

D_MODEL = 1024
CHUNK = 64
RMS_EPS = 1e-6
L2_EPS = 1e-6

S5_WIDTH = 512
S5_GROUP = 16
S5_GROUPS = 32
S5_P = 64
S5_STATE = S5_GROUPS * S5_P
S5_BLOCKS = 4
S5_BLOCK_STATE = S5_STATE // S5_BLOCKS

GDN_HEADS = 4
GDN_DK = 128
GDN_DV = 128
GDN_QK = GDN_HEADS * GDN_DK
GDN_V = GDN_HEADS * GDN_DV
GDN_CONV = 4
GDN_CONV_CH = 2 * GDN_QK + GDN_V

SWA_HEADS = 16
SWA_KV_HEADS = 4
SWA_GROUPS = 4
SWA_HD = 64
KV_WIN = 128
KV_COLS = 2 * SWA_KV_HEADS * SWA_HD

D_FF = 2816
FF_CHUNK = 256

PROJ_QKV = 0
PROJ_U = GDN_CONV_CH
PROJ_Z = PROJ_U + S5_WIDTH
PROJ_BA = PROJ_Z + GDN_V
PROJ_COLS = PROJ_BA + 128

LANES = 128
VMEM_LIMIT = 56 * 1024 * 1024


def _rms(x, w):
    return x * lax.rsqrt(jnp.mean(x * x, axis=-1, keepdims=True) + RMS_EPS) * w


def _dot(a, b):
    return jnp.dot(a.astype(BF16), b.astype(BF16), preferred_element_type=F32)


def _dot_nt(a, b):
    return lax.dot_general(a.astype(BF16), b.astype(BF16), (((1,), (1,)), ((), ())),
                           preferred_element_type=F32)


def _dot_tn(a, b):
    return lax.dot_general(a.astype(BF16), b.astype(BF16), (((0,), (0,)), ((), ())),
                           preferred_element_type=F32)


def _dot_exact(a, b):
    return jnp.dot(a, b, precision=lax.Precision.HIGHEST, preferred_element_type=F32)


def _const_spec(shape):
    return pl.BlockSpec(shape, lambda *_: (0,) * len(shape), pipeline_mode=pl.Buffered(1))


def _in_proj_kernel(x_ref, nw_ref, w_ref, o_ref):
    h = _rms(x_ref[...], nw_ref[...])
    o_ref[...] = _dot(h, w_ref[...])


def _in_proj(x2d, norm_w, w_perm, tm):
    t = x2d.shape[0]
    return pl.pallas_call(
        _in_proj_kernel,
        out_shape=jax.ShapeDtypeStruct((t, PROJ_COLS), F32),
        grid=(t // tm,),
        in_specs=[pl.BlockSpec((tm, D_MODEL), lambda i: (i, 0)),
                  _const_spec((1, D_MODEL)),
                  _const_spec((D_MODEL, PROJ_COLS))],
        out_specs=pl.BlockSpec((tm, PROJ_COLS), lambda i: (i, 0)),
        compiler_params=pltpu.CompilerParams(dimension_semantics=("arbitrary",),
                                             vmem_limit_bytes=VMEM_LIMIT),
        name="in_proj",
    )(x2d, norm_w, w_perm)


def _s5_kernel(u_ref, x0r_ref, x0i_ref, lamr_ref, lami_ref, wb_ref, wc_ref, d_ref, wglu_ref, bglu_ref,
               o_ref, xr_out, xi_out, st_r, st_i, xs, *, tc, bsz):
    i = pl.program_id(0)

    @pl.when(i == 0)
    def _():
        st_r[...] = x0r_ref[...]
        st_i[...] = x0i_ref[...]

    u_tm = jnp.swapaxes(u_ref[...], 0, 1).reshape(tc * bsz, S5_WIDTH)
    half = S5_BLOCK_STATE
    ys = []
    for j in range(S5_BLOCKS):
        bu = _dot(u_tm[:, j * LANES:(j + 1) * LANES], wb_ref[j])
        xs[...] = bu.reshape(tc, bsz, 2 * half)
        lr = jnp.broadcast_to(lamr_ref[:, j * half:(j + 1) * half], (bsz, half))
        li = jnp.broadcast_to(lami_ref[:, j * half:(j + 1) * half], (bsz, half))

        def step(t, carry, lr=lr, li=li):
            xr, xi = carry
            nr = lr * xr - li * xi + xs[t, :, :half]
            ni = lr * xi + li * xr + xs[t, :, half:]
            xs[t, :, :half] = nr
            xs[t, :, half:] = ni
            return nr, ni

        xr, xi = lax.fori_loop(0, tc, step, (st_r[:, j * half:(j + 1) * half], st_i[:, j * half:(j + 1) * half]),
                               unroll=8)
        st_r[:, j * half:(j + 1) * half] = xr
        st_i[:, j * half:(j + 1) * half] = xi
        ys.append(_dot(xs[...].reshape(tc * bsz, 2 * half), wc_ref[j]))
    y = jnp.concatenate(ys, axis=-1) + d_ref[...] * u_tm
    z = jax.nn.gelu(y)
    out = z * jax.nn.sigmoid(_dot(z, wglu_ref[...]) + bglu_ref[...])
    o_ref[...] = jnp.swapaxes(out.reshape(tc, bsz, S5_WIDTH), 0, 1)
    xr_out[...] = st_r[...]
    xi_out[...] = st_i[...]


def _s5(proj3, n_valid, x0r, x0i, lamr, lami, wb, wc, d_row, wglu, bglu_row, tc):
    bsz = proj3.shape[0]
    kern = functools.partial(_s5_kernel, tc=tc, bsz=bsz)
    return pl.pallas_call(
        kern,
        out_shape=(jax.ShapeDtypeStruct((bsz, n_valid, S5_WIDTH), F32),
                   jax.ShapeDtypeStruct((bsz, S5_STATE), F32),
                   jax.ShapeDtypeStruct((bsz, S5_STATE), F32)),
        grid=(n_valid // tc,),
        in_specs=[pl.BlockSpec((bsz, tc, S5_WIDTH), lambda i: (0, i, PROJ_U // S5_WIDTH)),
                  _const_spec((bsz, S5_STATE)), _const_spec((bsz, S5_STATE)),
                  _const_spec((1, S5_STATE)), _const_spec((1, S5_STATE)),
                  _const_spec((S5_BLOCKS, LANES, 2 * S5_BLOCK_STATE)),
                  _const_spec((S5_BLOCKS, 2 * S5_BLOCK_STATE, LANES)),
                  _const_spec((1, S5_WIDTH)), _const_spec((S5_WIDTH, S5_WIDTH)), _const_spec((1, S5_WIDTH))],
        out_specs=(pl.BlockSpec((bsz, tc, S5_WIDTH), lambda i: (0, i, 0)),
                   pl.BlockSpec((bsz, S5_STATE), lambda i: (0, 0)),
                   pl.BlockSpec((bsz, S5_STATE), lambda i: (0, 0))),
        scratch_shapes=[pltpu.VMEM((bsz, S5_STATE), F32), pltpu.VMEM((bsz, S5_STATE), F32),
                        pltpu.VMEM((tc, bsz, 2 * S5_BLOCK_STATE), F32)],
        compiler_params=pltpu.CompilerParams(dimension_semantics=("arbitrary",),
                                             vmem_limit_bytes=VMEM_LIMIT),
        name="s5_mixer",
    )(proj3, x0r, x0i, lamr, lami, wb, wc, d_row, wglu, bglu_row)


def _unit_lower_inverse(a_strict, eye):
    c = a_strict.shape[0]
    m = -a_strict
    p = eye + m
    width = 2
    while width < c:
        m = _dot_exact(m, m)
        p = p + _dot_exact(p, m)
        width *= 2
    return p


def _gdn_kernel(qkv_ref, z_ref, ba_ref, buf_ref, s0_ref, cw_ref, alog_ref, dtb_ref, nw_ref,
                o_ref, s_out, xp, s_st, *, tt, n_valid):
    i = pl.program_id(1)
    pad = 8

    @pl.when(i == 0)
    def _():
        xp[0:pad, :] = jnp.zeros((pad, GDN_CONV_CH), F32)
        xp[pad - (GDN_CONV - 1):pad, :] = buf_ref[...]
        s_st[...] = s0_ref[...]

    xp[pad:pad + tt, :] = qkv_ref[...]
    y = xp[pad:pad + tt, :] * cw_ref[GDN_CONV - 1:GDN_CONV, :]
    for j in range(GDN_CONV - 1):
        off = pad - (GDN_CONV - 1) + j
        y = y + xp[off:off + tt, :] * cw_ref[j:j + 1, :]
    xp[0:pad, :] = xp[tt:tt + pad, :]
    y = jax.nn.silu(y)

    ba = ba_ref[...]
    beta_all = jax.nn.sigmoid(ba)
    g_all = -jnp.exp(alog_ref[...]) * jax.nn.softplus(ba + dtb_ref[...])
    row = lax.broadcasted_iota(jnp.int32, (tt, LANES), 0)
    if n_valid % tt != 0:
        live = (i * tt + row) < n_valid
        beta_all = jnp.where(live, beta_all, 0.0)
        g_all = jnp.where(live, g_all, 0.0)
    pos = row % CHUNK
    gc_all = g_all
    shift = 1
    while shift < CHUNK:
        gc_all = gc_all + jnp.where(pos >= shift, pltpu.roll(gc_all, shift, axis=0), 0.0)
        shift *= 2
    gc_t = gc_all.T

    ri = lax.broadcasted_iota(jnp.int32, (CHUNK, CHUNK), 0)
    ci = lax.broadcasted_iota(jnp.int32, (CHUNK, CHUNK), 1)
    incl = ri >= ci
    strict = ri > ci
    eye = jnp.where(ri == ci, 1.0, 0.0).astype(F32)

    for h in range(GDN_HEADS):
        q_h = y[:, h * GDN_DK:(h + 1) * GDN_DK]
        k_h = y[:, GDN_QK + h * GDN_DK:GDN_QK + (h + 1) * GDN_DK]
        v_h = y[:, 2 * GDN_QK + h * GDN_DV:2 * GDN_QK + (h + 1) * GDN_DV]
        q_h = q_h * lax.rsqrt(jnp.sum(q_h * q_h, axis=-1, keepdims=True) + L2_EPS) * (GDN_DK ** -0.5)
        k_h = k_h * lax.rsqrt(jnp.sum(k_h * k_h, axis=-1, keepdims=True) + L2_EPS)
        s = s_st[h]
        outs = []
        for c in range(tt // CHUNK):
            r0 = c * CHUNK
            q = q_h[r0:r0 + CHUNK]
            k = k_h[r0:r0 + CHUNK]
            v = v_h[r0:r0 + CHUNK]
            beta = beta_all[r0:r0 + CHUNK, h:h + 1]
            gcol = gc_all[r0:r0 + CHUNK, GDN_HEADS + h:GDN_HEADS + h + 1]
            grow = gc_t[GDN_HEADS + h:GDN_HEADS + h + 1, r0:r0 + CHUNK]
            decay = jnp.where(incl, jnp.exp(jnp.where(incl, gcol - grow, 0.0)), 0.0)
            kb = k * beta
            a_mat = jnp.where(strict, _dot_nt(kb, k) * decay, 0.0)
            t_inv = _unit_lower_inverse(a_mat, eye)
            gexp = jnp.exp(gcol)
            uw = _dot(t_inv, jnp.concatenate([v * beta, kb * gexp], axis=-1))
            u_val = uw[:, :GDN_DV]
            w_key = uw[:, GDN_DV:]
            attn = _dot_nt(q, k) * decay
            g_end = gcol[CHUNK - 1:CHUNK, :]
            k_dec = k * jnp.exp(g_end - gcol)
            v_new = u_val - _dot(w_key, s)
            outs.append(_dot(q * gexp, s) + _dot(attn, v_new))
            s = s * jnp.exp(g_end) + _dot_tn(k_dec, v_new)
        s_st[h] = s
        o = jnp.concatenate(outs, axis=0)
        o = _rms(o, nw_ref[...])
        z_h = z_ref[:, h * GDN_DV:(h + 1) * GDN_DV]
        o_ref[:, h * GDN_DV:(h + 1) * GDN_DV] = o * jax.nn.silu(z_h)
    s_out[...] = s_st[...]


def _gdn(proj3, n_valid, conv_buf, s0, conv_w, alog_row, dtb_row, nw_row, tt):
    bsz, lp, _ = proj3.shape
    kern = functools.partial(_gdn_kernel, tt=tt, n_valid=n_valid)
    return pl.pallas_call(
        kern,
        out_shape=(jax.ShapeDtypeStruct((bsz, lp, GDN_V), F32),
                   jax.ShapeDtypeStruct((bsz, GDN_HEADS, GDN_DK, GDN_DV), F32)),
        grid=(bsz, lp // tt),
        in_specs=[pl.BlockSpec((None, tt, GDN_CONV_CH), lambda b, i: (b, i, PROJ_QKV // GDN_CONV_CH)),
                  pl.BlockSpec((None, tt, GDN_V), lambda b, i: (b, i, PROJ_Z // GDN_V)),
                  pl.BlockSpec((None, tt, LANES), lambda b, i: (b, i, PROJ_BA // LANES)),
                  pl.BlockSpec((None, GDN_CONV - 1, GDN_CONV_CH), lambda b, i: (b, 0, 0)),
                  pl.BlockSpec((None, GDN_HEADS, GDN_DK, GDN_DV), lambda b, i: (b, 0, 0, 0)),
                  _const_spec((GDN_CONV, GDN_CONV_CH)),
                  _const_spec((1, LANES)), _const_spec((1, LANES)), _const_spec((1, GDN_DV))],
        out_specs=(pl.BlockSpec((None, tt, GDN_V), lambda b, i: (b, i, 0)),
                   pl.BlockSpec((None, GDN_HEADS, GDN_DK, GDN_DV), lambda b, i: (b, 0, 0, 0))),
        scratch_shapes=[pltpu.VMEM((tt + 8, GDN_CONV_CH), F32),
                        pltpu.VMEM((GDN_HEADS, GDN_DK, GDN_DV), F32)],
        compiler_params=pltpu.CompilerParams(dimension_semantics=("arbitrary", "arbitrary"),
                                             vmem_limit_bytes=VMEM_LIMIT),
        name="gdn_mixer",
    )(proj3, proj3, proj3, conv_buf, s0, conv_w, alog_row, dtb_row, nw_row)


def _swiglu(x, nw_ref, wg_ref, wu_ref, wd_ref):
    h = _rms(x, nw_ref[...]).astype(BF16)
    acc = jnp.zeros(x.shape, F32)
    for f in range(0, D_FF, FF_CHUNK):
        g = jnp.dot(h, wg_ref[:, f:f + FF_CHUNK], preferred_element_type=F32)
        u = jnp.dot(h, wu_ref[:, f:f + FF_CHUNK], preferred_element_type=F32)
        acc = acc + _dot(jax.nn.silu(g) * u, wd_ref[f:f + FF_CHUNK, :])
    return x + acc


def _ffn_specs():
    return [_const_spec((1, D_MODEL)), _const_spec((D_MODEL, D_FF)), _const_spec((D_MODEL, D_FF)),
            _const_spec((D_FF, D_MODEL))]


def _mix0_kernel(x_ref, a_ref, b_ref, wo_ref, nw_ref, wg_ref, wu_ref, wd_ref, nw1_ref, wkv_ref,
                 o_ref, kv_ref):
    mix = _dot(a_ref[...], wo_ref[:S5_WIDTH, :]) + _dot(b_ref[...], wo_ref[S5_WIDTH:, :])
    x2 = _swiglu(x_ref[...] + mix, nw_ref, wg_ref, wu_ref, wd_ref)
    o_ref[...] = x2
    kv_ref[...] = _dot(_rms(x2, nw1_ref[...]), wkv_ref[...])


def _mix0(x2d, a2d, b2d, wo, nw, wg, wu, wd, nw1, wkv, tm):
    t = x2d.shape[0]
    return pl.pallas_call(
        _mix0_kernel,
        out_shape=(jax.ShapeDtypeStruct((t, D_MODEL), F32), jax.ShapeDtypeStruct((t, KV_COLS), F32)),
        grid=(t // tm,),
        in_specs=[pl.BlockSpec((tm, D_MODEL), lambda i: (i, 0)),
                  pl.BlockSpec((tm, S5_WIDTH), lambda i: (i, 0)),
                  pl.BlockSpec((tm, GDN_V), lambda i: (i, 0)),
                  _const_spec((S5_WIDTH + GDN_V, D_MODEL))] + _ffn_specs() +
                 [_const_spec((1, D_MODEL)), _const_spec((D_MODEL, KV_COLS))],
        out_specs=(pl.BlockSpec((tm, D_MODEL), lambda i: (i, 0)),
                   pl.BlockSpec((tm, KV_COLS), lambda i: (i, 0))),
        compiler_params=pltpu.CompilerParams(dimension_semantics=("arbitrary",),
                                             vmem_limit_bytes=VMEM_LIMIT),
        name="mix0_ffn0",
    )(x2d, a2d, b2d, wo, nw, wg, wu, wd, nw1, wkv)


def _swa_kernel(sink_ref, x_ref, kv_ref, kvp_ref, nwm_ref, wq_ref, wo_ref, nw_ref, wg_ref, wu_ref, wd_ref,
                nwf_ref, o_ref, kvc, att, *, tq, cq, n_valid, has_cache):
    i = pl.program_id(1)
    kvc[0:KV_WIN, :] = kvp_ref[...]
    kvc[KV_WIN:KV_WIN + tq, :] = kv_ref[...]
    x = x_ref[...]
    q = _dot(_rms(x, nwm_ref[...]), wq_ref[...]) * (SWA_HD ** -0.5)
    win = KV_WIN + cq
    half = SWA_KV_HEADS * SWA_HD
    col = lax.broadcasted_iota(jnp.int32, (SWA_GROUPS * cq, win), 1)
    grp = lax.broadcasted_iota(jnp.int32, (SWA_GROUPS * cq, 1), 0) // cq
    n_chunks = 1 if has_cache else tq // cq
    if n_chunks * cq < tq:
        att[...] = jnp.zeros((tq, D_MODEL), F32)
    for c in range(n_chunks):
        r0 = c * cq
        key_row = col + r0
        if has_cache:
            ok = key_row < KV_WIN + n_valid
        else:
            ok = jnp.logical_or(key_row >= KV_WIN, i > 0)
        for kh in range(SWA_KV_HEADS):
            qs = jnp.concatenate(
                [q[r0:r0 + cq, (kh * SWA_GROUPS + g) * SWA_HD:(kh * SWA_GROUPS + g + 1) * SWA_HD]
                 for g in range(SWA_GROUPS)], axis=0)
            ks = kvc[r0:r0 + win, kh * SWA_HD:(kh + 1) * SWA_HD]
            vs = kvc[r0:r0 + win, half + kh * SWA_HD:half + (kh + 1) * SWA_HD]
            sc = jnp.where(ok, _dot_nt(qs, ks), -1e30)
            sink = jnp.zeros((SWA_GROUPS * cq, 1), F32)
            for g in range(SWA_GROUPS):
                sink = jnp.where(grp == g, sink_ref[kh * SWA_GROUPS + g], sink)
            m = jnp.maximum(jnp.max(sc, axis=-1, keepdims=True), sink)
            p = jnp.exp(sc - m)
            den = jnp.sum(p, axis=-1, keepdims=True) + jnp.exp(sink - m)
            o = _dot(p / den, vs)
            for g in range(SWA_GROUPS):
                hq = kh * SWA_GROUPS + g
                att[r0:r0 + cq, hq * SWA_HD:(hq + 1) * SWA_HD] = o[g * cq:(g + 1) * cq]
    x3 = x + _dot(att[...], wo_ref[...])
    x4 = _swiglu(x3, nw_ref, wg_ref, wu_ref, wd_ref)
    o_ref[...] = _rms(x4, nwf_ref[...])


def _swa(sinks, x3d, kv3d, kv_prev, n_valid, has_cache, nwm, wq, wo, nw, wg, wu, wd, nwf, tq, cq):
    bsz, lp, _ = x3d.shape
    kern = functools.partial(_swa_kernel, tq=tq, cq=cq, n_valid=n_valid, has_cache=has_cache)
    per = tq // KV_WIN
    if has_cache:
        prev_map = lambda b, i, s: (b, 0, 0)
    else:
        prev_map = lambda b, i, s: (b, jnp.maximum(i * per - 1, 0), 0)
    grid_spec = pltpu.PrefetchScalarGridSpec(
        num_scalar_prefetch=1,
        grid=(bsz, lp // tq),
        in_specs=[pl.BlockSpec((None, tq, D_MODEL), lambda b, i, s: (b, i, 0)),
                  pl.BlockSpec((None, tq, KV_COLS), lambda b, i, s: (b, i, 0)),
                  pl.BlockSpec((None, KV_WIN, KV_COLS), prev_map),
                  _const_spec((1, D_MODEL)), _const_spec((D_MODEL, D_MODEL)), _const_spec((D_MODEL, D_MODEL))]
                 + _ffn_specs() + [_const_spec((1, D_MODEL))],
        out_specs=pl.BlockSpec((None, tq, D_MODEL), lambda b, i, s: (b, i, 0)),
        scratch_shapes=[pltpu.VMEM((KV_WIN + tq, KV_COLS), F32), pltpu.VMEM((tq, D_MODEL), F32)],
    )
    return pl.pallas_call(
        kern,
        out_shape=jax.ShapeDtypeStruct((bsz, lp, D_MODEL), F32),
        grid_spec=grid_spec,
        compiler_params=pltpu.CompilerParams(dimension_semantics=("arbitrary", "arbitrary"),
                                             vmem_limit_bytes=VMEM_LIMIT),
        name="swa_ffn1",
    )(sinks, x3d, kv3d, kv_prev, nwm, wq, wo, nw, wg, wu, wd, nwf)


def _prep_params(p):
    q = {}
    w_in = p["w_in"][0]
    off_qkv = S5_WIDTH
    off_z = off_qkv + GDN_CONV_CH
    off_b = off_z + GDN_V
    w_perm = jnp.concatenate(
        [w_in[:, off_qkv:off_z], w_in[:, :S5_WIDTH], w_in[:, off_z:off_b], w_in[:, off_b:],
         jnp.zeros((D_MODEL, LANES - 2 * GDN_HEADS), F32)], axis=1)
    q["w_in"] = w_perm.astype(BF16)

    lam_re, lam_im = p["s5_lam_re"][0], p["s5_lam_im"][0]
    dt = jnp.exp(p["s5_log_dt"][0])[:, None]
    mag = jnp.exp(lam_re * dt)
    lbr, lbi = mag * jnp.cos(lam_im * dt), mag * jnp.sin(lam_im * dt)
    den = lam_re * lam_re + lam_im * lam_im
    nr, ni = lbr - 1.0, lbi
    cr = (nr * lam_re + ni * lam_im) / den
    ci = (ni * lam_re - nr * lam_im) / den
    b_re, b_im = p["s5_b_re"][0], p["s5_b_im"][0]
    bbr = cr[..., None] * b_re - ci[..., None] * b_im
    bbi = cr[..., None] * b_im + ci[..., None] * b_re
    gb = S5_GROUPS // S5_BLOCKS
    eye = jnp.eye(gb, dtype=F32)

    def in_blocks(m):
        m = m.reshape(S5_BLOCKS, gb, S5_P, S5_GROUP)
        return jnp.einsum("jgpc,gh->jgchp", m, eye).reshape(S5_BLOCKS, gb * S5_GROUP, gb * S5_P)

    def out_blocks(m):
        m = m.reshape(S5_BLOCKS, gb, S5_GROUP, S5_P)
        return jnp.einsum("jgcp,gh->jgphc", m, eye).reshape(S5_BLOCKS, gb * S5_P, gb * S5_GROUP)

    q["s5_wb"] = jnp.concatenate([in_blocks(bbr), in_blocks(bbi)], axis=2).astype(BF16)
    q["s5_wc"] = jnp.concatenate([out_blocks(p["s5_c_re"][0]), -out_blocks(p["s5_c_im"][0])], axis=1).astype(BF16)
    q["s5_lamr"] = lbr.reshape(1, S5_STATE)
    q["s5_lami"] = lbi.reshape(1, S5_STATE)
    q["s5_d"] = p["s5_d"][0].reshape(1, S5_WIDTH)
    q["s5_wglu"] = p["s5_w_glu"][0].astype(BF16)
    q["s5_bglu"] = p["s5_b_glu"][0].reshape(1, S5_WIDTH)

    def head_row(v):
        return jnp.zeros((1, LANES), F32).at[0, GDN_HEADS:2 * GDN_HEADS].set(v)

    q["gdn_alog"] = head_row(p["gdn_a_log"][0])
    q["gdn_dtb"] = head_row(p["gdn_dt_bias"][0])
    q["gdn_conv_w"] = p["gdn_conv_w"][0]
    q["gdn_nw"] = p["gdn_norm_w"][0].reshape(1, GDN_DV)
    q["w_out"] = p["w_out_ab"][0].astype(BF16)
    q["wq"] = p["swa_wq"][0].astype(BF16)
    q["wkv"] = jnp.concatenate([p["swa_wk"][0], p["swa_wv"][0]], axis=1).astype(BF16)
    q["wo"] = p["swa_wo"][0].astype(BF16)
    q["sinks"] = p["swa_sinks"][0]
    for name in ("norm_mix", "norm_ffn"):
        q[name] = p[name].reshape(-1, 1, D_MODEL)
    q["norm_final"] = p["norm_final"].reshape(1, D_MODEL)
    for name in ("ffn_w_gate", "ffn_w_up", "ffn_w_down"):
        q[name] = p[name].astype(BF16)
    return q


def _tile(n, target):
    t = min(n, target)
    while n % t:
        t //= 2
    return t


def _trunk(x, s5_re, s5_im, gdn_s, gdn_conv, cache_k, cache_v, q, n_valid, tiles):
    bsz, lp, _ = x.shape
    has_cache = cache_k is not None
    t = bsz * lp
    tm = _tile(t, tiles["tm"])
    x2d = x.reshape(t, D_MODEL)

    proj = _in_proj(x2d, q["norm_mix"][0], q["w_in"], tm).reshape(bsz, lp, PROJ_COLS)
    a_out, x_re, x_im = _s5(proj, n_valid, s5_re.reshape(bsz, S5_STATE), s5_im.reshape(bsz, S5_STATE),
                            q["s5_lamr"], q["s5_lami"], q["s5_wb"], q["s5_wc"], q["s5_d"], q["s5_wglu"],
                            q["s5_bglu"], _tile(n_valid, tiles["tc"]))
    if n_valid != lp:
        a_out = jnp.pad(a_out, ((0, 0), (0, lp - n_valid), (0, 0)))
    b_out, s_fin = _gdn(proj, n_valid, gdn_conv, gdn_s, q["gdn_conv_w"], q["gdn_alog"], q["gdn_dtb"],
                        q["gdn_nw"], _tile(lp, tiles["tt"]))
    qkv = proj[:, :, PROJ_QKV:PROJ_QKV + GDN_CONV_CH]
    new_buf = jnp.concatenate([gdn_conv, qkv[:, :n_valid]], axis=1)[:, -(GDN_CONV - 1):]

    x2, kv = _mix0(x2d, a_out.reshape(t, S5_WIDTH), b_out.reshape(t, GDN_V), q["w_out"], q["norm_ffn"][0],
                   q["ffn_w_gate"][0], q["ffn_w_up"][0], q["ffn_w_down"][0], q["norm_mix"][1], q["wkv"], tm)
    kv = kv.reshape(bsz, lp, KV_COLS)
    half = SWA_KV_HEADS * SWA_HD
    if has_cache:
        kv_prev = jnp.concatenate([cache_k.reshape(bsz, KV_WIN, half), cache_v.reshape(bsz, KV_WIN, half)], axis=-1)
        tq, cq = lp, n_valid
    else:
        kv_prev = kv
        tq, cq = _tile(lp, tiles["tq"]), CHUNK
    y = _swa(q["sinks"], x2.reshape(bsz, lp, D_MODEL), kv, kv_prev, n_valid, has_cache, q["norm_mix"][1],
             q["wq"], q["wo"], q["norm_ffn"][1], q["ffn_w_gate"][1], q["ffn_w_up"][1], q["ffn_w_down"][1],
             q["norm_final"], tq, cq)

    k_new = kv[:, :n_valid, :half]
    v_new = kv[:, :n_valid, half:]
    if has_cache:
        k_new = jnp.concatenate([cache_k.reshape(bsz, KV_WIN, half), k_new], axis=1)
        v_new = jnp.concatenate([cache_v.reshape(bsz, KV_WIN, half), v_new], axis=1)
    ck = k_new[:, -KV_WIN:].reshape(bsz, KV_WIN, SWA_KV_HEADS, SWA_HD)
    cv = v_new[:, -KV_WIN:].reshape(bsz, KV_WIN, SWA_KV_HEADS, SWA_HD)
    return (y[:, :n_valid], x_re.reshape(1, bsz, S5_GROUPS, S5_P), x_im.reshape(1, bsz, S5_GROUPS, S5_P),
            s_fin[None], new_buf[None], ck[None], cv[None])


PROMPT_TILES = dict(tm=512, tc=64, tt=256, tq=256)
SAMPLE_PAD = 128


def kernel(x_prompt, x_sample, state_s5_re, state_s5_im, state_gdn, state_gdn_conv, cache_swa_k, cache_swa_v,
           norm_mix, norm_ffn, norm_final, w_in, s5_lam_re, s5_lam_im, s5_log_dt, s5_b_re, s5_b_im,
           s5_c_re, s5_c_im, s5_d, s5_w_glu, s5_b_glu, gdn_conv_w, gdn_a_log, gdn_dt_bias, gdn_norm_w,
           w_out_ab, swa_wq, swa_wk, swa_wv, swa_sinks, swa_wo, ffn_w_gate, ffn_w_up, ffn_w_down):
    p = dict(norm_mix=norm_mix, norm_ffn=norm_ffn, norm_final=norm_final, w_in=w_in,
             s5_lam_re=s5_lam_re, s5_lam_im=s5_lam_im, s5_log_dt=s5_log_dt, s5_b_re=s5_b_re, s5_b_im=s5_b_im,
             s5_c_re=s5_c_re, s5_c_im=s5_c_im, s5_d=s5_d, s5_w_glu=s5_w_glu, s5_b_glu=s5_b_glu,
             gdn_conv_w=gdn_conv_w, gdn_a_log=gdn_a_log, gdn_dt_bias=gdn_dt_bias, gdn_norm_w=gdn_norm_w,
             w_out_ab=w_out_ab, swa_wq=swa_wq, swa_wk=swa_wk, swa_wv=swa_wv, swa_sinks=swa_sinks,
             swa_wo=swa_wo, ffn_w_gate=ffn_w_gate, ffn_w_up=ffn_w_up, ffn_w_down=ffn_w_down)
    q = _prep_params(p)
    bsz, seq, _ = x_prompt.shape
    dbsz, dseq, _ = x_sample.shape
    zeros_s5 = jnp.zeros((bsz, S5_GROUPS, S5_P), F32)
    outs_p = _trunk(x_prompt, zeros_s5, zeros_s5, jnp.zeros((bsz, GDN_HEADS, GDN_DK, GDN_DV), F32),
                    jnp.zeros((bsz, GDN_CONV - 1, GDN_CONV_CH), F32), None, None, q, seq, PROMPT_TILES)
    x_s = jnp.pad(x_sample, ((0, 0), (0, SAMPLE_PAD - dseq), (0, 0)))
    outs_s = _trunk(x_s, state_s5_re[0], state_s5_im[0], state_gdn[0], state_gdn_conv[0],
                    cache_swa_k[0], cache_swa_v[0], q, dseq, PROMPT_TILES)
    return (outs_p[0], outs_s[0]) + tuple(outs_p[1:]) + tuple(outs_s[1:])
```

```python
import functools
import math

import jax
import jax.numpy as jnp
from jax import lax
from jax.experimental import pallas as pl
from jax.experimental.pallas import tpu as pltpu

F32 = jnp.float32
BF16 = jnp.bfloat16

D_MODEL = 1024
CHUNK = 64
RMS_EPS = 1e-6
L2_EPS = 1e-6

S5_WIDTH = 512
S5_GROUP = 16
S5_GROUPS = 32
S5_P = 64
S5_STATE = S5_GROUPS * S5_P
S5_BLOCKS = 4
S5_BLOCK_STATE = S5_STATE // S5_BLOCKS

GDN_HEADS = 4
GDN_DK = 128
GDN_DV = 128
GDN_QK = GDN_HEADS * GDN_DK
GDN_V = GDN_HEADS * GDN_DV
GDN_CONV = 4
GDN_CONV_CH = 2 * GDN_QK + GDN_V

SWA_HEADS = 16
SWA_KV_HEADS = 4
SWA_GROUPS = 4
SWA_HD = 64
KV_WIN = 128
KV_COLS = 2 * SWA_KV_HEADS * SWA_HD
SWA_STAGE_CHUNKS = 2

D_FF = 2816
FF_CHUNK = 256

PROJ_QKV = 0
PROJ_U = GDN_CONV_CH
PROJ_Z = PROJ_U + S5_WIDTH
PROJ_BA = PROJ_Z + GDN_V
PROJ_COLS = PROJ_BA + 128

LANES = 128
VMEM_LIMIT = 56 * 1024 * 1024


def _rms(x, w):
    return x * lax.rsqrt(jnp.mean(x * x, axis=-1, keepdims=True) + RMS_EPS) * w


def _dot(a, b):
    return jnp.dot(a.astype(BF16), b.astype(BF16), preferred_element_type=F32)


def _dot_nt(a, b):
    return lax.dot_general(a.astype(BF16), b.astype(BF16), (((1,), (1,)), ((), ())),
                           preferred_element_type=F32)


def _const_spec(shape):
    return pl.BlockSpec(shape, lambda *_: (0,) * len(shape), pipeline_mode=pl.Buffered(1))


def _in_proj_kernel(x_ref, nw_ref, w_ref, o_ref):
    h = _rms(x_ref[...], nw_ref[...])
    o_ref[...] = _dot(h, w_ref[...])


def _in_proj(x2d, norm_w, w_perm, tm):
    t = x2d.shape[0]
    return pl.pallas_call(
        _in_proj_kernel,
        out_shape=jax.ShapeDtypeStruct((t, PROJ_COLS), F32),
        grid=(t // tm,),
        in_specs=[pl.BlockSpec((tm, D_MODEL), lambda i: (i, 0)),
                  _const_spec((1, D_MODEL)),
                  _const_spec((D_MODEL, PROJ_COLS))],
        out_specs=pl.BlockSpec((tm, PROJ_COLS), lambda i: (i, 0)),
        compiler_params=pltpu.CompilerParams(dimension_semantics=("arbitrary",),
                                             vmem_limit_bytes=VMEM_LIMIT),
        name="in_proj",
    )(x2d, norm_w, w_perm)


def _s5_kernel(u_ref, x0r_ref, x0i_ref, lamr_ref, lami_ref, wb_ref, wc_ref, d_ref, wglu_ref, bglu_ref,
               o_ref, xr_out, xi_out, st_r, st_i, xs, *, tc, bsz):
    i = pl.program_id(0)

    @pl.when(i == 0)
    def _():
        st_r[...] = x0r_ref[...]
        st_i[...] = x0i_ref[...]

    u_tm = jnp.swapaxes(u_ref[...], 0, 1).reshape(tc * bsz, S5_WIDTH)
    half = S5_BLOCK_STATE
    ys = []
    for j in range(S5_BLOCKS):
        bu = _dot(u_tm[:, j * LANES:(j + 1) * LANES], wb_ref[j])
        xs[...] = bu.reshape(tc, bsz, 2 * half)
        lr = jnp.broadcast_to(lamr_ref[:, j * half:(j + 1) * half], (bsz, half))
        li = jnp.broadcast_to(lami_ref[:, j * half:(j + 1) * half], (bsz, half))

        def step(t, carry, lr=lr, li=li):
            xr, xi = carry
            nr = lr * xr - li * xi + xs[t, :, :half]
            ni = lr * xi + li * xr + xs[t, :, half:]
            xs[t, :, :half] = nr
            xs[t, :, half:] = ni
            return nr, ni

        xr, xi = lax.fori_loop(0, tc, step, (st_r[:, j * half:(j + 1) * half], st_i[:, j * half:(j + 1) * half]),
                               unroll=8)
        st_r[:, j * half:(j + 1) * half] = xr
        st_i[:, j * half:(j + 1) * half] = xi
        ys.append(_dot(xs[...].reshape(tc * bsz, 2 * half), wc_ref[j]))
    y = jnp.concatenate(ys, axis=-1) + d_ref[...] * u_tm
    z = jax.nn.gelu(y)
    out = z * jax.nn.sigmoid(_dot(z, wglu_ref[...]) + bglu_ref[...])
    o_ref[...] = jnp.swapaxes(out.reshape(tc, bsz, S5_WIDTH), 0, 1)
    xr_out[...] = st_r[...]
    xi_out[...] = st_i[...]


def _s5(proj3, n_valid, x0r, x0i, lamr, lami, wb, wc, d_row, wglu, bglu_row, tc):
    bsz = proj3.shape[0]
    kern = functools.partial(_s5_kernel, tc=tc, bsz=bsz)
    return pl.pallas_call(
        kern,
        out_shape=(jax.ShapeDtypeStruct((bsz, n_valid, S5_WIDTH), F32),
                   jax.ShapeDtypeStruct((bsz, S5_STATE), F32),
                   jax.ShapeDtypeStruct((bsz, S5_STATE), F32)),
        grid=(n_valid // tc,),
        in_specs=[pl.BlockSpec((bsz, tc, S5_WIDTH), lambda i: (0, i, PROJ_U // S5_WIDTH)),
                  _const_spec((bsz, S5_STATE)), _const_spec((bsz, S5_STATE)),
                  _const_spec((1, S5_STATE)), _const_spec((1, S5_STATE)),
                  _const_spec((S5_BLOCKS, LANES, 2 * S5_BLOCK_STATE)),
                  _const_spec((S5_BLOCKS, 2 * S5_BLOCK_STATE, LANES)),
                  _const_spec((1, S5_WIDTH)), _const_spec((S5_WIDTH, S5_WIDTH)), _const_spec((1, S5_WIDTH))],
        out_specs=(pl.BlockSpec((bsz, tc, S5_WIDTH), lambda i: (0, i, 0)),
                   pl.BlockSpec((bsz, S5_STATE), lambda i: (0, 0)),
                   pl.BlockSpec((bsz, S5_STATE), lambda i: (0, 0))),
        scratch_shapes=[pltpu.VMEM((bsz, S5_STATE), F32), pltpu.VMEM((bsz, S5_STATE), F32),
                        pltpu.VMEM((tc, bsz, 2 * S5_BLOCK_STATE), F32)],
        compiler_params=pltpu.CompilerParams(dimension_semantics=("arbitrary",),
                                             vmem_limit_bytes=VMEM_LIMIT),
        name="s5_mixer",
    )(proj3, x0r, x0i, lamr, lami, wb, wc, d_row, wglu, bglu_row)


def _solve_unit_lower(a_strict, rhs, block):
    n = len(rhs)
    w = rhs[0].shape[1]

    def hi_lo(x):
        hi = x.astype(BF16)
        return [hi, (x - hi.astype(F32)).astype(BF16)]

    m = [(-a).astype(BF16) for a in a_strict]
    x = list(rhs)
    width = 1
    while 2 * width < block:
        p = [jnp.dot(m[j], jnp.concatenate(hi_lo(x[j]) + [m[j]], axis=1), preferred_element_type=F32)
             for j in range(n)]
        x = [x[j] + (p[j][:, :w] + p[j][:, w:2 * w]) for j in range(n)]
        m = [p[j][:, 2 * w:].astype(BF16) for j in range(n)]
        width *= 2
    p = [jnp.dot(m[j], jnp.concatenate(hi_lo(x[j]), axis=1), preferred_element_type=F32) for j in range(n)]
    return [x[j] + (p[j][:, :w] + p[j][:, w:]) for j in range(n)]


def _gdn_kernel(qkv_ref, z_ref, ba_ref, buf_ref, s0_ref, cw_ref, alog_ref, dtb_ref, nw_ref,
                o_ref, s_out, xp, s_st, *, tt, n_valid):
    i = pl.program_id(1)
    pad = 8
    nc = tt // CHUNK

    @pl.when(i == 0)
    def _():
        xp[0:pad, :] = jnp.zeros((pad, GDN_CONV_CH), F32)
        xp[pad - (GDN_CONV - 1):pad, :] = buf_ref[...]
        s_st[...] = s0_ref[...]

    xp[pad:pad + tt, :] = qkv_ref[...]
    y = xp[pad:pad + tt, :] * cw_ref[GDN_CONV - 1:GDN_CONV, :]
    for j in range(GDN_CONV - 1):
        off = pad - (GDN_CONV - 1) + j
        y = y + xp[off:off + tt, :] * cw_ref[j:j + 1, :]
    xp[0:pad, :] = xp[tt:tt + pad, :]
    y = jax.nn.silu(y)

    ba = ba_ref[...]
    beta_all = jax.nn.sigmoid(ba)
    g_all = -jnp.exp(alog_ref[...]) * jax.nn.softplus(ba + dtb_ref[...])
    row = lax.broadcasted_iota(jnp.int32, (tt, LANES), 0)
    if n_valid % tt != 0:
        live = (i * tt + row) < n_valid
        beta_all = jnp.where(live, beta_all, 0.0)
        g_all = jnp.where(live, g_all, 0.0)
    pos = row % CHUNK
    gc_all = g_all
    shift = 1
    while shift < CHUNK:
        gc_all = gc_all + jnp.where(pos >= shift, pltpu.roll(gc_all, shift, axis=0), 0.0)
        shift *= 2
    g_end_all = jnp.concatenate(
        [jnp.broadcast_to(gc_all[c * CHUNK + CHUNK - 1:(c + 1) * CHUNK, :], (CHUNK, LANES)) for c in range(nc)],
        axis=0)
    gc_t = gc_all.T
    rest_t = (g_end_all - gc_all).T

    ri = lax.broadcasted_iota(jnp.int32, (tt, tt), 0)
    ci = lax.broadcasted_iota(jnp.int32, (tt, tt), 1)
    same = (ri // CHUNK) == (ci // CHUNK)
    incl = jnp.logical_and(same, ri >= ci)
    strict = jnp.logical_and(same, ri > ci)

    heads = range(GDN_HEADS)
    kq, kb, kt, vb, gexp, qn = [], [], [], [], [], []
    for h in heads:
        q_h = y[:, h * GDN_DK:(h + 1) * GDN_DK]
        k_h = y[:, GDN_QK + h * GDN_DK:GDN_QK + (h + 1) * GDN_DK]
        v_h = y[:, 2 * GDN_QK + h * GDN_DV:2 * GDN_QK + (h + 1) * GDN_DV]
        q_h = q_h * (lax.rsqrt(jnp.sum(q_h * q_h, axis=-1, keepdims=True) + L2_EPS) * (GDN_DK ** -0.5))
        k_h = k_h * lax.rsqrt(jnp.sum(k_h * k_h, axis=-1, keepdims=True) + L2_EPS)
        beta_b = jnp.broadcast_to(beta_all[:, h:h + 1], (tt, GDN_DK))
        kb.append(k_h * beta_b)
        vb.append(v_h * beta_b)
        kt.append(k_h.T)
        qn.append(q_h)
        kq.append(_dot(jnp.concatenate([kb[h], q_h], axis=0), kt[h]))
    a_mat, attn, rhs, q_dec, kdt = [], [], [], [], []
    for h in heads:
        gc_b = jnp.broadcast_to(gc_all[:, GDN_HEADS + h:GDN_HEADS + h + 1], (tt, tt))
        gexp_b = jnp.exp(gc_b[:, :GDN_DK])
        grow = gc_t[GDN_HEADS + h:GDN_HEADS + h + 1, :]
        decay = jnp.where(incl, jnp.exp(jnp.where(incl, gc_b - grow, 0.0)), 0.0)
        a_mat.append(jnp.where(strict, kq[h][:tt] * decay, 0.0))
        attn.append(kq[h][tt:] * decay)
        rhs.append(jnp.concatenate([vb[h], kb[h] * gexp_b], axis=1))
        q_dec.append(qn[h] * gexp_b)
        kdt.append(kt[h] * jnp.exp(rest_t[GDN_HEADS + h:GDN_HEADS + h + 1, :]))
    uw = _solve_unit_lower(a_mat, rhs, CHUNK)
    u_val = [uw[h][:, :GDN_DV] for h in heads]
    w_key = [uw[h][:, GDN_DV:] for h in heads]

    s = [s_st[h] for h in heads]
    o_state = [[] for _ in heads]
    v_new = [[] for _ in heads]
    for c in range(nc):
        rows = slice(c * CHUNK, (c + 1) * CHUNK)
        wq = [_dot(jnp.concatenate([w_key[h][rows], q_dec[h][rows]], axis=0), s[h]) for h in heads]
        for h in heads:
            o_state[h].append(wq[h][CHUNK:])
            v_new[h].append(u_val[h][rows] - wq[h][:CHUNK])
        upd = [_dot(kdt[h][:, rows], v_new[h][c]) for h in heads]
        for h in heads:
            g_last = jnp.exp(g_end_all[c * CHUNK:c * CHUNK + 1, GDN_HEADS + h:GDN_HEADS + h + 1])
            s[h] = s[h] * g_last + upd[h]
    o_all = [jnp.concatenate(o_state[h], axis=0) + _dot(attn[h], jnp.concatenate(v_new[h], axis=0)) for h in heads]
    for h in heads:
        s_st[h] = s[h]
        o = _rms(o_all[h], nw_ref[...])
        z_h = z_ref[:, h * GDN_DV:(h + 1) * GDN_DV]
        o_ref[:, h * GDN_DV:(h + 1) * GDN_DV] = o * jax.nn.silu(z_h)
    s_out[...] = s_st[...]


def _gdn(proj3, n_valid, conv_buf, s0, conv_w, alog_row, dtb_row, nw_row, tt):
    bsz, lp, _ = proj3.shape
    kern = functools.partial(_gdn_kernel, tt=tt, n_valid=n_valid)
    return pl.pallas_call(
        kern,
        out_shape=(jax.ShapeDtypeStruct((bsz, lp, GDN_V), F32),
                   jax.ShapeDtypeStruct((bsz, GDN_HEADS, GDN_DK, GDN_DV), F32)),
        grid=(bsz, lp // tt),
        in_specs=[pl.BlockSpec((None, tt, GDN_CONV_CH), lambda b, i: (b, i, PROJ_QKV // GDN_CONV_CH)),
                  pl.BlockSpec((None, tt, GDN_V), lambda b, i: (b, i, PROJ_Z // GDN_V)),
                  pl.BlockSpec((None, tt, LANES), lambda b, i: (b, i, PROJ_BA // LANES)),
                  pl.BlockSpec((None, GDN_CONV - 1, GDN_CONV_CH), lambda b, i: (b, 0, 0)),
                  pl.BlockSpec((None, GDN_HEADS, GDN_DK, GDN_DV), lambda b, i: (b, 0, 0, 0)),
                  _const_spec((GDN_CONV, GDN_CONV_CH)),
                  _const_spec((1, LANES)), _const_spec((1, LANES)), _const_spec((1, GDN_DV))],
        out_specs=(pl.BlockSpec((None, tt, GDN_V), lambda b, i: (b, i, 0)),
                   pl.BlockSpec((None, GDN_HEADS, GDN_DK, GDN_DV), lambda b, i: (b, 0, 0, 0))),
        scratch_shapes=[pltpu.VMEM((tt + 8, GDN_CONV_CH), F32),
                        pltpu.VMEM((GDN_HEADS, GDN_DK, GDN_DV), F32)],
        compiler_params=pltpu.CompilerParams(dimension_semantics=("arbitrary", "arbitrary"),
                                             vmem_limit_bytes=VMEM_LIMIT),
        name="gdn_mixer",
    )(proj3, proj3, proj3, conv_buf, s0, conv_w, alog_row, dtb_row, nw_row)


def _swiglu(x, nw_ref, wg_ref, wu_ref, wd_ref):
    h = _rms(x, nw_ref[...]).astype(BF16)
    acts = []
    for f in range(0, D_FF, FF_CHUNK):
        g = jnp.dot(h, wg_ref[:, f:f + FF_CHUNK], preferred_element_type=F32)
        u = jnp.dot(h, wu_ref[:, f:f + FF_CHUNK], preferred_element_type=F32)
        acts.append((jax.nn.silu(g) * u).astype(BF16))
    return x + jnp.dot(jnp.concatenate(acts, axis=1), wd_ref[...], preferred_element_type=F32)


def _ffn_specs():
    return [_const_spec((1, D_MODEL)), _const_spec((D_MODEL, D_FF)), _const_spec((D_MODEL, D_FF)),
            _const_spec((D_FF, D_MODEL))]


def _mix0_kernel(x_ref, a_ref, b_ref, wo_ref, nw_ref, wg_ref, wu_ref, wd_ref, nw1_ref, wkv_ref,
                 o_ref, kv_ref):
    mix = _dot(a_ref[...], wo_ref[:S5_WIDTH, :]) + _dot(b_ref[...], wo_ref[S5_WIDTH:, :])
    x2 = _swiglu(x_ref[...] + mix, nw_ref, wg_ref, wu_ref, wd_ref)
    o_ref[...] = x2
    kv_ref[...] = _dot(_rms(x2, nw1_ref[...]), wkv_ref[...])


def _mix0(x2d, a2d, b2d, wo, nw, wg, wu, wd, nw1, wkv, tm):
    t = x2d.shape[0]
    return pl.pallas_call(
        _mix0_kernel,
        out_shape=(jax.ShapeDtypeStruct((t, D_MODEL), F32), jax.ShapeDtypeStruct((t, KV_COLS), F32)),
        grid=(t // tm,),
        in_specs=[pl.BlockSpec((tm, D_MODEL), lambda i: (i, 0)),
                  pl.BlockSpec((tm, S5_WIDTH), lambda i: (i, 0)),
                  pl.BlockSpec((tm, GDN_V), lambda i: (i, 0)),
                  _const_spec((S5_WIDTH + GDN_V, D_MODEL))] + _ffn_specs() +
                 [_const_spec((1, D_MODEL)), _const_spec((D_MODEL, KV_COLS))],
        out_specs=(pl.BlockSpec((tm, D_MODEL), lambda i: (i, 0)),
                   pl.BlockSpec((tm, KV_COLS), lambda i: (i, 0))),
        compiler_params=pltpu.CompilerParams(dimension_semantics=("arbitrary",),
                                             vmem_limit_bytes=VMEM_LIMIT),
        name="mix0_ffn0",
    )(x2d, a2d, b2d, wo, nw, wg, wu, wd, nw1, wkv)


def _swa_kernel(sink_ref, x_ref, kv_ref, kvp_ref, nwm_ref, wq_ref, wo_ref, nw_ref, wg_ref, wu_ref, wd_ref,
                nwf_ref, o_ref, kvc, att, *, tq, cq, n_valid, has_cache):
    i = pl.program_id(1)
    kvc[0:KV_WIN, :] = kvp_ref[...]
    kvc[KV_WIN:KV_WIN + tq, :] = kv_ref[...]
    x = x_ref[...]
    q = (_dot(_rms(x, nwm_ref[...]), wq_ref[...]) * (SWA_HD ** -0.5)).astype(BF16)
    win = KV_WIN + cq
    rows_all = KV_WIN + tq
    low = lax.broadcasted_iota(jnp.int32, (rows_all, LANES), 1) < SWA_HD
    k_lo, k_hi, v_lo, v_hi = [], [], [], []
    for dst_lo, dst_hi, base in ((k_lo, k_hi, 0), (v_lo, v_hi, SWA_KV_HEADS * SWA_HD)):
        for j in range(SWA_KV_HEADS // 2):
            t = kvc[:, base + j * LANES:base + (j + 1) * LANES]
            r = pltpu.roll(t, SWA_HD, axis=1)
            dst_lo += [jnp.where(low, t, 0.0).astype(BF16), jnp.where(low, r, 0.0).astype(BF16)]
            dst_hi += [jnp.where(low, 0.0, r).astype(BF16), jnp.where(low, 0.0, t).astype(BF16)]
    col = lax.broadcasted_iota(jnp.int32, (2 * cq, win), 1)
    top = lax.broadcasted_iota(jnp.int32, (2 * cq, 1), 0) < cq
    n_chunks = 1 if has_cache else tq // cq
    if n_chunks * cq < tq:
        att[...] = jnp.zeros((tq, D_MODEL), F32)

    def soft(sc, sink):
        m = jnp.maximum(jnp.max(sc, axis=-1, keepdims=True), sink)
        p = jnp.exp(sc - m)
        return p.astype(BF16), 1.0 / (jnp.sum(p, axis=-1, keepdims=True) + jnp.exp(sink - m))

    for c0 in range(0, n_chunks, SWA_STAGE_CHUNKS):
        units = [(c, kh) for c in range(c0, min(c0 + SWA_STAGE_CHUNKS, n_chunks)) for kh in range(SWA_KV_HEADS)]
        scores = []
        for c, kh in units:
            r0 = c * cq
            qs = jnp.concatenate([q[r0:r0 + cq, (2 * kh) * LANES:(2 * kh + 1) * LANES],
                                  q[r0:r0 + cq, (2 * kh + 1) * LANES:(2 * kh + 2) * LANES]], axis=0)
            keys = slice(r0, r0 + win)
            scores.append((_dot_nt(qs, k_lo[kh][keys]), _dot_nt(qs, k_hi[kh][keys])))
        probs = []
        for (c, kh), (s_e, s_o) in zip(units, scores):
            key_row = col + c * cq
            if has_cache:
                ok = key_row < KV_WIN + n_valid
            else:
                ok = jnp.logical_or(key_row >= KV_WIN, i > 0)
            h0 = kh * SWA_GROUPS
            probs.append(soft(jnp.where(ok, s_e, -1e30), jnp.where(top, sink_ref[h0], sink_ref[h0 + 2]))
                         + soft(jnp.where(ok, s_o, -1e30), jnp.where(top, sink_ref[h0 + 1], sink_ref[h0 + 3])))
        for (c, kh), (p_e, inv_e, p_o, inv_o) in zip(units, probs):
            r0 = c * cq
            keys = slice(r0, r0 + win)
            o = inv_e * jnp.dot(p_e, v_lo[kh][keys], preferred_element_type=F32) \
                + inv_o * jnp.dot(p_o, v_hi[kh][keys], preferred_element_type=F32)
            att[r0:r0 + cq, (2 * kh) * LANES:(2 * kh + 1) * LANES] = o[:cq]
            att[r0:r0 + cq, (2 * kh + 1) * LANES:(2 * kh + 2) * LANES] = o[cq:]
    x3 = x + _dot(att[...], wo_ref[...])
    x4 = _swiglu(x3, nw_ref, wg_ref, wu_ref, wd_ref)
    o_ref[...] = _rms(x4, nwf_ref[...])


def _swa(sinks, x3d, kv3d, kv_prev, n_valid, has_cache, nwm, wq, wo, nw, wg, wu, wd, nwf, tq, cq):
    bsz, lp, _ = x3d.shape
    kern = functools.partial(_swa_kernel, tq=tq, cq=cq, n_valid=n_valid, has_cache=has_cache)
    per = tq // KV_WIN
    if has_cache:
        prev_map = lambda b, i, s: (b, 0, 0)
    else:
        prev_map = lambda b, i, s: (b, jnp.maximum(i * per - 1, 0), 0)
    grid_spec = pltpu.PrefetchScalarGridSpec(
        num_scalar_prefetch=1,
        grid=(bsz, lp // tq),
        in_specs=[pl.BlockSpec((None, tq, D_MODEL), lambda b, i, s: (b, i, 0)),
                  pl.BlockSpec((None, tq, KV_COLS), lambda b, i, s: (b, i, 0)),
                  pl.BlockSpec((None, KV_WIN, KV_COLS), prev_map),
                  _const_spec((1, D_MODEL)), _const_spec((D_MODEL, D_MODEL)), _const_spec((D_MODEL, D_MODEL))]
                 + _ffn_specs() + [_const_spec((1, D_MODEL))],
        out_specs=pl.BlockSpec((None, tq, D_MODEL), lambda b, i, s: (b, i, 0)),
        scratch_shapes=[pltpu.VMEM((KV_WIN + tq, KV_COLS), F32), pltpu.VMEM((tq, D_MODEL), F32)],
    )
    return pl.pallas_call(
        kern,
        out_shape=jax.ShapeDtypeStruct((bsz, lp, D_MODEL), F32),
        grid_spec=grid_spec,
        compiler_params=pltpu.CompilerParams(dimension_semantics=("arbitrary", "arbitrary"),
                                             vmem_limit_bytes=VMEM_LIMIT),
        name="swa_ffn1",
    )(sinks, x3d, kv3d, kv_prev, nwm, wq, wo, nw, wg, wu, wd, nwf)


def _prep_params(p):
    q = {}
    w_in = p["w_in"][0]
    off_qkv = S5_WIDTH
    off_z = off_qkv + GDN_CONV_CH
    off_b = off_z + GDN_V
    w_perm = jnp.concatenate(
        [w_in[:, off_qkv:off_z], w_in[:, :S5_WIDTH], w_in[:, off_z:off_b], w_in[:, off_b:],
         jnp.zeros((D_MODEL, LANES - 2 * GDN_HEADS), F32)], axis=1)
    q["w_in"] = w_perm.astype(BF16)

    lam_re, lam_im = p["s5_lam_re"][0], p["s5_lam_im"][0]
    dt = jnp.exp(p["s5_log_dt"][0])[:, None]
    mag = jnp.exp(lam_re * dt)
    lbr, lbi = mag * jnp.cos(lam_im * dt), mag * jnp.sin(lam_im * dt)
    den = lam_re * lam_re + lam_im * lam_im
    nr, ni = lbr - 1.0, lbi
    cr = (nr * lam_re + ni * lam_im) / den
    ci = (ni * lam_re - nr * lam_im) / den
    b_re, b_im = p["s5_b_re"][0], p["s5_b_im"][0]
    bbr = cr[..., None] * b_re - ci[..., None] * b_im
    bbi = cr[..., None] * b_im + ci[..., None] * b_re
    gb = S5_GROUPS // S5_BLOCKS
    eye = jnp.eye(gb, dtype=F32)

    def in_blocks(m):
        m = m.reshape(S5_BLOCKS, gb, S5_P, S5_GROUP)
        return jnp.einsum("jgpc,gh->jgchp", m, eye).reshape(S5_BLOCKS, gb * S5_GROUP, gb * S5_P)

    def out_blocks(m):
        m = m.reshape(S5_BLOCKS, gb, S5_GROUP, S5_P)
        return jnp.einsum("jgcp,gh->jgphc", m, eye).reshape(S5_BLOCKS, gb * S5_P, gb * S5_GROUP)

    q["s5_wb"] = jnp.concatenate([in_blocks(bbr), in_blocks(bbi)], axis=2).astype(BF16)
    q["s5_wc"] = jnp.concatenate([out_blocks(p["s5_c_re"][0]), -out_blocks(p["s5_c_im"][0])], axis=1).astype(BF16)
    q["s5_lamr"] = lbr.reshape(1, S5_STATE)
    q["s5_lami"] = lbi.reshape(1, S5_STATE)
    q["s5_d"] = p["s5_d"][0].reshape(1, S5_WIDTH)
    q["s5_wglu"] = p["s5_w_glu"][0].astype(BF16)
    q["s5_bglu"] = p["s5_b_glu"][0].reshape(1, S5_WIDTH)

    def head_row(v):
        return jnp.zeros((1, LANES), F32).at[0, GDN_HEADS:2 * GDN_HEADS].set(v)

    q["gdn_alog"] = head_row(p["gdn_a_log"][0])
    q["gdn_dtb"] = head_row(p["gdn_dt_bias"][0])
    q["gdn_conv_w"] = p["gdn_conv_w"][0]
    q["gdn_nw"] = p["gdn_norm_w"][0].reshape(1, GDN_DV)
    q["w_out"] = p["w_out_ab"][0].astype(BF16)
    q["wq"] = p["swa_wq"][0].astype(BF16)
    q["wkv"] = jnp.concatenate([p["swa_wk"][0], p["swa_wv"][0]], axis=1).astype(BF16)
    q["wo"] = p["swa_wo"][0].astype(BF16)
    q["sinks"] = p["swa_sinks"][0]
    for name in ("norm_mix", "norm_ffn"):
        q[name] = p[name].reshape(-1, 1, D_MODEL)
    q["norm_final"] = p["norm_final"].reshape(1, D_MODEL)
    for name in ("ffn_w_gate", "ffn_w_up", "ffn_w_down"):
        q[name] = p[name].astype(BF16)
    return q


def _tile(n, target):
    t = min(n, target)
    while n % t:
        t //= 2
    return t


def _trunk(x, s5_re, s5_im, gdn_s, gdn_conv, cache_k, cache_v, q, n_valid, tiles):
    bsz, lp, _ = x.shape
    has_cache = cache_k is not None
    t = bsz * lp
    tm = _tile(t, tiles["tm"])
    x2d = x.reshape(t, D_MODEL)

    proj = _in_proj(x2d, q["norm_mix"][0], q["w_in"], tm).reshape(bsz, lp, PROJ_COLS)
    a_out, x_re, x_im = _s5(proj, n_valid, s5_re.reshape(bsz, S5_STATE), s5_im.reshape(bsz, S5_STATE),
                            q["s5_lamr"], q["s5_lami"], q["s5_wb"], q["s5_wc"], q["s5_d"], q["s5_wglu"],
                            q["s5_bglu"], _tile(n_valid, tiles["tc"]))
    if n_valid != lp:
        a_out = jnp.pad(a_out, ((0, 0), (0, lp - n_valid), (0, 0)))
    b_out, s_fin = _gdn(proj, n_valid, gdn_conv, gdn_s, q["gdn_conv_w"], q["gdn_alog"], q["gdn_dtb"],
                        q["gdn_nw"], _tile(lp, tiles["tt"]))
    qkv = proj[:, :, PROJ_QKV:PROJ_QKV + GDN_CONV_CH]
    new_buf = jnp.concatenate([gdn_conv, qkv[:, :n_valid]], axis=1)[:, -(GDN_CONV - 1):]

    x2, kv = _mix0(x2d, a_out.reshape(t, S5_WIDTH), b_out.reshape(t, GDN_V), q["w_out"], q["norm_ffn"][0],
                   q["ffn_w_gate"][0], q["ffn_w_up"][0], q["ffn_w_down"][0], q["norm_mix"][1], q["wkv"], tm)
    kv = kv.reshape(bsz, lp, KV_COLS)
    half = SWA_KV_HEADS * SWA_HD
    if has_cache:
        kv_prev = jnp.concatenate([cache_k.reshape(bsz, KV_WIN, half), cache_v.reshape(bsz, KV_WIN, half)], axis=-1)
        tq, cq = lp, n_valid
    else:
        kv_prev = kv
        tq, cq = _tile(lp, tiles["tq"]), CHUNK
    y = _swa(q["sinks"], x2.reshape(bsz, lp, D_MODEL), kv, kv_prev, n_valid, has_cache, q["norm_mix"][1],
             q["wq"], q["wo"], q["norm_ffn"][1], q["ffn_w_gate"][1], q["ffn_w_up"][1], q["ffn_w_down"][1],
             q["norm_final"], tq, cq)

    k_new = kv[:, :n_valid, :half]
    v_new = kv[:, :n_valid, half:]
    if has_cache:
        k_new = jnp.concatenate([cache_k.reshape(bsz, KV_WIN, half), k_new], axis=1)
        v_new = jnp.concatenate([cache_v.reshape(bsz, KV_WIN, half), v_new], axis=1)
    ck = k_new[:, -KV_WIN:].reshape(bsz, KV_WIN, SWA_KV_HEADS, SWA_HD)
    cv = v_new[:, -KV_WIN:].reshape(bsz, KV_WIN, SWA_KV_HEADS, SWA_HD)
    return (y[:, :n_valid], x_re.reshape(1, bsz, S5_GROUPS, S5_P), x_im.reshape(1, bsz, S5_GROUPS, S5_P),
            s_fin[None], new_buf[None], ck[None], cv[None])


PROMPT_TILES = dict(tm=512, tc=64, tt=256, tq=512)
SAMPLE_PAD = 128


def kernel(x_prompt, x_sample, state_s5_re, state_s5_im, state_gdn, state_gdn_conv, cache_swa_k, cache_swa_v,
           norm_mix, norm_ffn, norm_final, w_in, s5_lam_re, s5_lam_im, s5_log_dt, s5_b_re, s5_b_im,
           s5_c_re, s5_c_im, s5_d, s5_w_glu, s5_b_glu, gdn_conv_w, gdn_a_log, gdn_dt_bias, gdn_norm_w,
           w_out_ab, swa_wq, swa_wk, swa_wv, swa_sinks, swa_wo, ffn_w_gate, ffn_w_up, ffn_w_down):
    p = dict(norm_mix=norm_mix, norm_ffn=norm_ffn, norm_final=norm_final, w_in=w_in,
             s5_lam_re=s5_lam_re, s5_lam_im=s5_lam_im, s5_log_dt=s5_log_dt, s5_b_re=s5_b_re, s5_b_im=s5_b_im,
             s5_c_re=s5_c_re, s5_c_im=s5_c_im, s5_d=s5_d, s5_w_glu=s5_w_glu, s5_b_glu=s5_b_glu,
             gdn_conv_w=gdn_conv_w, gdn_a_log=gdn_a_log, gdn_dt_bias=gdn_dt_bias, gdn_norm_w=gdn_norm_w,
             w_out_ab=w_out_ab, swa_wq=swa_wq, swa_wk=swa_wk, swa_wv=swa_wv, swa_sinks=swa_sinks,
             swa_wo=swa_wo, ffn_w_gate=ffn_w_gate, ffn_w_up=ffn_w_up, ffn_w_down=ffn_w_down)
    q = _prep_params(p)
    bsz, seq, _ = x_prompt.shape
    dbsz, dseq, _ = x_sample.shape
    zeros_s5 = jnp.zeros((bsz, S5_GROUPS, S5_P), F32)
    outs_p = _trunk(x_prompt, zeros_s5, zeros_s5, jnp.zeros((bsz, GDN_HEADS, GDN_DK, GDN_DV), F32),
                    jnp.zeros((bsz, GDN_CONV - 1, GDN_CONV_CH), F32), None, None, q, seq, PROMPT_TILES)
    x_s = jnp.pad(x_sample, ((0, 0), (0, SAMPLE_PAD - dseq), (0, 0)))
    outs_s = _trunk(x_s, state_s5_re[0], state_s5_im[0], state_gdn[0], state_gdn_conv[0],
                    cache_swa_k[0], cache_swa_v[0], q, dseq, PROMPT_TILES)
    return (outs_p[0], outs_s[0]) + tuple(outs_p[1:]) + tuple(outs_s[1:])
```

```python
import functools
import math

import jax
import jax.numpy as jnp
from jax import lax
from jax.experimental import pallas as pl
from jax.experimental.pallas import tpu as pltpu

F32 = jnp.float32
BF16 = jnp.bfloat16

D_MODEL = 1024
CHUNK = 64
RMS_EPS = 1e-6
L2_EPS = 1e-6

S5_WIDTH = 512
S5_GROUP = 16
S5_GROUPS = 32
S5_P = 64
S5_STATE = S5_GROUPS * S5_P
S5_BLOCKS = 4
S5_BLOCK_STATE = S5_STATE // S5_BLOCKS

GDN_HEADS = 4
GDN_DK = 128
GDN_DV = 128
GDN_QK = GDN_HEADS * GDN_DK
GDN_V = GDN_HEADS * GDN_DV
GDN_CONV = 4
GDN_CONV_CH = 2 * GDN_QK + GDN_V
GDN_ROWS_PER_STEP = 2
TAIL_ROWS = 8

SWA_HEADS = 16
SWA_KV_HEADS = 4
SWA_GROUPS = 4
SWA_HD = 64
KV_WIN = 128
KV_COLS = 2 * SWA_KV_HEADS * SWA_HD
SWA_STAGE_CHUNKS = 2

D_FF = 2816
FF_CHUNK = 256

PROJ_QKV = 0
PROJ_U = GDN_CONV_CH
PROJ_Z = PROJ_U + S5_WIDTH
PROJ_BA = PROJ_Z + GDN_V
PROJ_COLS = PROJ_BA + 128

LANES = 128
VMEM_LIMIT = 56 * 1024 * 1024


def _rms(x, w):
    return x * lax.rsqrt(jnp.mean(x * x, axis=-1, keepdims=True) + RMS_EPS) * w


def _dot(a, b):
    return jnp.dot(a.astype(BF16), b.astype(BF16), preferred_element_type=F32)


def _dot_nt(a, b):
    return lax.dot_general(a.astype(BF16), b.astype(BF16), (((1,), (1,)), ((), ())),
                           preferred_element_type=F32)


def _const_spec(shape):
    return pl.BlockSpec(shape, lambda *_: (0,) * len(shape), pipeline_mode=pl.Buffered(1))


def _in_proj_kernel(x_ref, nw_ref, w_ref, cw_ref, buf_ref, o_ref, tail_ref, xp, *, tm, tiles_per_seq, tail_tile,
                    tail_off):
    it = pl.program_id(0) % tiles_per_seq
    pad = TAIL_ROWS

    @pl.when(it == 0)
    def _():
        xp[0:pad, :] = jnp.zeros((pad, GDN_CONV_CH), F32)
        xp[pad - (GDN_CONV - 1):pad, :] = buf_ref[...]

    p = _dot(_rms(x_ref[...], nw_ref[...]), w_ref[...])
    o_ref[:, GDN_CONV_CH:] = p[:, GDN_CONV_CH:]
    xp[pad:pad + tm, :] = p[:, :GDN_CONV_CH]

    y = xp[pad:pad + tm, :] * cw_ref[GDN_CONV - 1:GDN_CONV, :]
    for j in range(GDN_CONV - 1):
        off = pad - (GDN_CONV - 1) + j
        y = y + xp[off:off + tm, :] * cw_ref[j:j + 1, :]
    xp[0:pad, :] = xp[tm:tm + pad, :]
    y = jax.nn.silu(y)
    for h in range(2 * GDN_HEADS):
        cols = slice(h * GDN_DK, (h + 1) * GDN_DK)
        x_h = y[:, cols]
        scale = GDN_DK ** -0.5 if h < GDN_HEADS else 1.0
        o_ref[:, cols] = x_h * (lax.rsqrt(jnp.sum(x_h * x_h, axis=-1, keepdims=True) + L2_EPS) * scale)
    o_ref[:, 2 * GDN_QK:GDN_CONV_CH] = y[:, 2 * GDN_QK:]

    @pl.when(it == tail_tile)
    def _():
        tail_ref[...] = xp[pad + tail_off:pad + tail_off + TAIL_ROWS, :]


def _in_proj(x3d, n_valid, norm_w, w_perm, conv_w, conv_buf, tm):
    bsz, lp, _ = x3d.shape
    t = bsz * lp
    tiles_per_seq = lp // tm
    assert n_valid >= TAIL_ROWS and (n_valid - TAIL_ROWS) // tm == (n_valid - 1) // tm
    tail_tile = (n_valid - 1) // tm
    kern = functools.partial(_in_proj_kernel, tm=tm, tiles_per_seq=tiles_per_seq, tail_tile=tail_tile,
                             tail_off=n_valid - TAIL_ROWS - tail_tile * tm)
    return pl.pallas_call(
        kern,
        out_shape=(jax.ShapeDtypeStruct((t, PROJ_COLS), F32),
                   jax.ShapeDtypeStruct((bsz, TAIL_ROWS, GDN_CONV_CH), F32)),
        grid=(t // tm,),
        in_specs=[pl.BlockSpec((tm, D_MODEL), lambda i: (i, 0)),
                  _const_spec((1, D_MODEL)),
                  _const_spec((D_MODEL, PROJ_COLS)),
                  _const_spec((GDN_CONV, GDN_CONV_CH)),
                  pl.BlockSpec((None, GDN_CONV - 1, GDN_CONV_CH), lambda i: (i // tiles_per_seq, 0, 0))],
        out_specs=(pl.BlockSpec((tm, PROJ_COLS), lambda i: (i, 0)),
                   pl.BlockSpec((None, TAIL_ROWS, GDN_CONV_CH), lambda i: (i // tiles_per_seq, 0, 0))),
        scratch_shapes=[pltpu.VMEM((tm + TAIL_ROWS, GDN_CONV_CH), F32)],
        compiler_params=pltpu.CompilerParams(dimension_semantics=("arbitrary",),
                                             vmem_limit_bytes=VMEM_LIMIT),
        name="in_proj",
    )(x3d.reshape(t, D_MODEL), norm_w, w_perm, conv_w, conv_buf)


def _s5_kernel(u_ref, x0r_ref, x0i_ref, lamr_ref, lami_ref, wb_ref, wc_ref, d_ref, wglu_ref, bglu_ref,
               o_ref, xr_out, xi_out, st_r, st_i, xs, *, tc, bsz):
    i = pl.program_id(0)

    @pl.when(i == 0)
    def _():
        st_r[...] = x0r_ref[...]
        st_i[...] = x0i_ref[...]

    u_tm = jnp.swapaxes(u_ref[...], 0, 1).reshape(tc * bsz, S5_WIDTH)
    half = S5_BLOCK_STATE
    ys = []
    for j in range(S5_BLOCKS):
        bu = _dot(u_tm[:, j * LANES:(j + 1) * LANES], wb_ref[j])
        xs[j] = bu.reshape(tc, bsz, 2 * half)
        lr = jnp.broadcast_to(lamr_ref[:, j * half:(j + 1) * half], (bsz, half))
        li = jnp.broadcast_to(lami_ref[:, j * half:(j + 1) * half], (bsz, half))
        xr = st_r[:, j * half:(j + 1) * half]
        xi = st_i[:, j * half:(j + 1) * half]
        for t in range(tc):
            xr, xi = (lr * xr - li * xi + xs[j, t, :, :half], lr * xi + li * xr + xs[j, t, :, half:])
            xs[j, t, :, :half] = xr
            xs[j, t, :, half:] = xi
        st_r[:, j * half:(j + 1) * half] = xr
        st_i[:, j * half:(j + 1) * half] = xi
        ys.append(_dot(xs[j].reshape(tc * bsz, 2 * half), wc_ref[j]))
    y = jnp.concatenate(ys, axis=-1) + d_ref[...] * u_tm
    z = jax.nn.gelu(y)
    out = z * jax.nn.sigmoid(_dot(z, wglu_ref[...]) + bglu_ref[...])
    o_ref[...] = jnp.swapaxes(out.reshape(tc, bsz, S5_WIDTH), 0, 1)
    xr_out[...] = st_r[...]
    xi_out[...] = st_i[...]


def _s5(proj3, n_valid, x0r, x0i, lamr, lami, wb, wc, d_row, wglu, bglu_row, tc):
    bsz = proj3.shape[0]
    kern = functools.partial(_s5_kernel, tc=tc, bsz=bsz)
    return pl.pallas_call(
        kern,
        out_shape=(jax.ShapeDtypeStruct((bsz, n_valid, S5_WIDTH), F32),
                   jax.ShapeDtypeStruct((bsz, S5_STATE), F32),
                   jax.ShapeDtypeStruct((bsz, S5_STATE), F32)),
        grid=(n_valid // tc,),
        in_specs=[pl.BlockSpec((bsz, tc, S5_WIDTH), lambda i: (0, i, PROJ_U // S5_WIDTH)),
                  _const_spec((bsz, S5_STATE)), _const_spec((bsz, S5_STATE)),
                  _const_spec((1, S5_STATE)), _const_spec((1, S5_STATE)),
                  _const_spec((S5_BLOCKS, LANES, 2 * S5_BLOCK_STATE)),
                  _const_spec((S5_BLOCKS, 2 * S5_BLOCK_STATE, LANES)),
                  _const_spec((1, S5_WIDTH)), _const_spec((S5_WIDTH, S5_WIDTH)), _const_spec((1, S5_WIDTH))],
        out_specs=(pl.BlockSpec((bsz, tc, S5_WIDTH), lambda i: (0, i, 0)),
                   pl.BlockSpec((bsz, S5_STATE), lambda i: (0, 0)),
                   pl.BlockSpec((bsz, S5_STATE), lambda i: (0, 0))),
        scratch_shapes=[pltpu.VMEM((bsz, S5_STATE), F32), pltpu.VMEM((bsz, S5_STATE), F32),
                        pltpu.VMEM((S5_BLOCKS, tc, bsz, 2 * S5_BLOCK_STATE), F32)],
        compiler_params=pltpu.CompilerParams(dimension_semantics=("arbitrary",),
                                             vmem_limit_bytes=VMEM_LIMIT),
        name="s5_mixer",
    )(proj3, x0r, x0i, lamr, lami, wb, wc, d_row, wglu, bglu_row)


def _solve_unit_lower(a_strict, rhs, block):
    n = len(rhs)
    w = rhs[0].shape[1]

    def hi_lo(x):
        hi = x.astype(BF16)
        return [hi, (x - hi.astype(F32)).astype(BF16)]

    m = [(-a).astype(BF16) for a in a_strict]
    x = list(rhs)
    width = 1
    while 2 * width < block:
        p = [jnp.dot(m[j], jnp.concatenate(hi_lo(x[j]) + [m[j]], axis=1), preferred_element_type=F32)
             for j in range(n)]
        x = [x[j] + (p[j][:, :w] + p[j][:, w:2 * w]) for j in range(n)]
        m = [p[j][:, 2 * w:].astype(BF16) for j in range(n)]
        width *= 2
    p = [jnp.dot(m[j], jnp.concatenate(hi_lo(x[j]), axis=1), preferred_element_type=F32) for j in range(n)]
    return [x[j] + (p[j][:, :w] + p[j][:, w:]) for j in range(n)]


def _gdn_kernel(qkv_ref, z_ref, ba_ref, s0_ref, alog_ref, dtb_ref, nw_ref, o_ref, s_out, s_st, *, tt, n_valid,
                rows_per_step):
    @pl.when(pl.program_id(1) == 0)
    def _():
        s_st[...] = s0_ref[...]

    for r in range(rows_per_step):
        _gdn_stream(qkv_ref.at[r], z_ref.at[r], ba_ref.at[r], alog_ref, dtb_ref, nw_ref,
                    o_ref.at[r], s_out.at[r], s_st.at[r], tt=tt, n_valid=n_valid)


def _gdn_stream(qkv_ref, z_ref, ba_ref, alog_ref, dtb_ref, nw_ref, o_ref, s_out, s_st, *, tt, n_valid):
    i = pl.program_id(1)
    nc = tt // CHUNK
    y = qkv_ref[...]

    ba = ba_ref[...]
    beta_all = jax.nn.sigmoid(ba)
    g_all = -jnp.exp(alog_ref[...]) * jax.nn.softplus(ba + dtb_ref[...])
    row = lax.broadcasted_iota(jnp.int32, (tt, LANES), 0)
    if n_valid % tt != 0:
        live = (i * tt + row) < n_valid
        beta_all = jnp.where(live, beta_all, 0.0)
        g_all = jnp.where(live, g_all, 0.0)
    pos = row % CHUNK
    gc_all = g_all
    shift = 1
    while shift < CHUNK:
        gc_all = gc_all + jnp.where(pos >= shift, pltpu.roll(gc_all, shift, axis=0), 0.0)
        shift *= 2
    g_end_all = jnp.concatenate(
        [jnp.broadcast_to(gc_all[c * CHUNK + CHUNK - 1:(c + 1) * CHUNK, :], (CHUNK, LANES)) for c in range(nc)],
        axis=0)
    gc_t = gc_all.T
    rest_t = (g_end_all - gc_all).T

    ri = lax.broadcasted_iota(jnp.int32, (tt, tt), 0)
    ci = lax.broadcasted_iota(jnp.int32, (tt, tt), 1)
    same = (ri // CHUNK) == (ci // CHUNK)
    incl = jnp.logical_and(same, ri >= ci)
    strict = jnp.logical_and(same, ri > ci)

    heads = range(GDN_HEADS)
    kq, kb, kt, vb, qn = [], [], [], [], []
    for h in heads:
        q_h = y[:, h * GDN_DK:(h + 1) * GDN_DK]
        k_h = y[:, GDN_QK + h * GDN_DK:GDN_QK + (h + 1) * GDN_DK]
        v_h = y[:, 2 * GDN_QK + h * GDN_DV:2 * GDN_QK + (h + 1) * GDN_DV]
        beta_b = jnp.broadcast_to(beta_all[:, h:h + 1], (tt, GDN_DK))
        kb.append(k_h * beta_b)
        vb.append(v_h * beta_b)
        kt.append(k_h.T)
        qn.append(q_h)
        kq.append(_dot(jnp.concatenate([kb[h], q_h], axis=0), kt[h]))
    a_mat, attn, rhs, q_dec, kdt = [], [], [], [], []
    for h in heads:
        gc_b = jnp.broadcast_to(gc_all[:, GDN_HEADS + h:GDN_HEADS + h + 1], (tt, max(tt, GDN_DK)))
        gexp_b = jnp.exp(gc_b[:, :GDN_DK])
        grow = gc_t[GDN_HEADS + h:GDN_HEADS + h + 1, :]
        decay = jnp.where(incl, jnp.exp(jnp.where(incl, gc_b[:, :tt] - grow, 0.0)), 0.0)
        a_mat.append(jnp.where(strict, kq[h][:tt] * decay, 0.0))
        attn.append(kq[h][tt:] * decay)
        rhs.append(jnp.concatenate([vb[h], kb[h] * gexp_b], axis=1))
        q_dec.append(qn[h] * gexp_b)
        kdt.append(kt[h] * jnp.exp(rest_t[GDN_HEADS + h:GDN_HEADS + h + 1, :]))
    uw = _solve_unit_lower(a_mat, rhs, CHUNK)
    u_val = [uw[h][:, :GDN_DV] for h in heads]
    w_key = [uw[h][:, GDN_DV:] for h in heads]

    s = [s_st[h] for h in heads]
    o_state = [[] for _ in heads]
    v_new = [[] for _ in heads]
    for c in range(nc):
        rows = slice(c * CHUNK, (c + 1) * CHUNK)
        wq = [_dot(jnp.concatenate([w_key[h][rows], q_dec[h][rows]], axis=0), s[h]) for h in heads]
        for h in heads:
            o_state[h].append(wq[h][CHUNK:])
            v_new[h].append(u_val[h][rows] - wq[h][:CHUNK])
        upd = [_dot(kdt[h][:, rows], v_new[h][c]) for h in heads]
        for h in heads:
            g_last = jnp.exp(g_end_all[c * CHUNK:c * CHUNK + 1, GDN_HEADS + h:GDN_HEADS + h + 1])
            s[h] = s[h] * g_last + upd[h]
    o_all = [jnp.concatenate(o_state[h], axis=0) + _dot(attn[h], jnp.concatenate(v_new[h], axis=0)) for h in heads]
    for h in heads:
        s_st[h] = s[h]
        o = _rms(o_all[h], nw_ref[...])
        z_h = z_ref[:, h * GDN_DV:(h + 1) * GDN_DV]
        o_ref[:, h * GDN_DV:(h + 1) * GDN_DV] = o * jax.nn.silu(z_h)
    s_out[...] = s_st[...]


def _gdn(proj3, n_valid, s0, alog_row, dtb_row, nw_row, tt):
    bsz, lp, _ = proj3.shape
    rps = GDN_ROWS_PER_STEP
    kern = functools.partial(_gdn_kernel, tt=tt, n_valid=n_valid, rows_per_step=rps)
    return pl.pallas_call(
        kern,
        out_shape=(jax.ShapeDtypeStruct((bsz, lp, GDN_V), F32),
                   jax.ShapeDtypeStruct((bsz, GDN_HEADS, GDN_DK, GDN_DV), F32)),
        grid=(bsz // rps, lp // tt),
        in_specs=[pl.BlockSpec((rps, tt, GDN_CONV_CH), lambda b, i: (b, i, PROJ_QKV // GDN_CONV_CH)),
                  pl.BlockSpec((rps, tt, GDN_V), lambda b, i: (b, i, PROJ_Z // GDN_V)),
                  pl.BlockSpec((rps, tt, LANES), lambda b, i: (b, i, PROJ_BA // LANES)),
                  pl.BlockSpec((rps, GDN_HEADS, GDN_DK, GDN_DV), lambda b, i: (b, 0, 0, 0)),
                  _const_spec((1, LANES)), _const_spec((1, LANES)), _const_spec((1, GDN_DV))],
        out_specs=(pl.BlockSpec((rps, tt, GDN_V), lambda b, i: (b, i, 0)),
                   pl.BlockSpec((rps, GDN_HEADS, GDN_DK, GDN_DV), lambda b, i: (b, 0, 0, 0))),
        scratch_shapes=[pltpu.VMEM((rps, GDN_HEADS, GDN_DK, GDN_DV), F32)],
        compiler_params=pltpu.CompilerParams(dimension_semantics=("arbitrary", "arbitrary"),
                                             vmem_limit_bytes=VMEM_LIMIT),
        name="gdn_mixer",
    )(proj3, proj3, proj3, s0, alog_row, dtb_row, nw_row)


def _swiglu(x, nw_ref, wg_ref, wu_ref, wd_ref):
    h = _rms(x, nw_ref[...]).astype(BF16)
    acts = []
    for f in range(0, D_FF, FF_CHUNK):
        g = jnp.dot(h, wg_ref[:, f:f + FF_CHUNK], preferred_element_type=F32)
        u = jnp.dot(h, wu_ref[:, f:f + FF_CHUNK], preferred_element_type=F32)
        acts.append((jax.nn.silu(g) * u).astype(BF16))
    return x + jnp.dot(jnp.concatenate(acts, axis=1), wd_ref[...], preferred_element_type=F32)


def _ffn_specs():
    return [_const_spec((1, D_MODEL)), _const_spec((D_MODEL, D_FF)), _const_spec((D_MODEL, D_FF)),
            _const_spec((D_FF, D_MODEL))]


def _mix0_kernel(x_ref, a_ref, b_ref, wo_ref, nw_ref, wg_ref, wu_ref, wd_ref, nw1_ref, wkv_ref,
                 o_ref, kv_ref):
    mix = _dot(a_ref[...], wo_ref[:S5_WIDTH, :]) + _dot(b_ref[...], wo_ref[S5_WIDTH:, :])
    x2 = _swiglu(x_ref[...] + mix, nw_ref, wg_ref, wu_ref, wd_ref)
    o_ref[...] = x2
    kv_ref[...] = _dot(_rms(x2, nw1_ref[...]), wkv_ref[...])


def _mix0(x2d, a2d, b2d, wo, nw, wg, wu, wd, nw1, wkv, tm):
    t = x2d.shape[0]
    return pl.pallas_call(
        _mix0_kernel,
        out_shape=(jax.ShapeDtypeStruct((t, D_MODEL), F32), jax.ShapeDtypeStruct((t, KV_COLS), F32)),
        grid=(t // tm,),
        in_specs=[pl.BlockSpec((tm, D_MODEL), lambda i: (i, 0)),
                  pl.BlockSpec((tm, S5_WIDTH), lambda i: (i, 0)),
                  pl.BlockSpec((tm, GDN_V), lambda i: (i, 0)),
                  _const_spec((S5_WIDTH + GDN_V, D_MODEL))] + _ffn_specs() +
                 [_const_spec((1, D_MODEL)), _const_spec((D_MODEL, KV_COLS))],
        out_specs=(pl.BlockSpec((tm, D_MODEL), lambda i: (i, 0)),
                   pl.BlockSpec((tm, KV_COLS), lambda i: (i, 0))),
        compiler_params=pltpu.CompilerParams(dimension_semantics=("arbitrary",),
                                             vmem_limit_bytes=VMEM_LIMIT),
        name="mix0_ffn0",
    )(x2d, a2d, b2d, wo, nw, wg, wu, wd, nw1, wkv)


def _swa_kernel(sink_ref, x_ref, kv_ref, kvp_ref, nwm_ref, wq_ref, wo_ref, nw_ref, wg_ref, wu_ref, wd_ref,
                nwf_ref, o_ref, kvc, att, *, tq, cq, n_valid, has_cache):
    i = pl.program_id(1)
    kvc[0:KV_WIN, :] = kvp_ref[...]
    kvc[KV_WIN:KV_WIN + tq, :] = kv_ref[...]
    x = x_ref[...]
    q = (_dot(_rms(x, nwm_ref[...]), wq_ref[...]) * (SWA_HD ** -0.5)).astype(BF16)
    win = KV_WIN + cq
    rows_all = KV_WIN + tq
    low = lax.broadcasted_iota(jnp.int32, (rows_all, LANES), 1) < SWA_HD
    k_lo, k_hi, v_lo, v_hi = [], [], [], []
    for dst_lo, dst_hi, base in ((k_lo, k_hi, 0), (v_lo, v_hi, SWA_KV_HEADS * SWA_HD)):
        for j in range(SWA_KV_HEADS // 2):
            t = kvc[:, base + j * LANES:base + (j + 1) * LANES]
            r = pltpu.roll(t, SWA_HD, axis=1)
            dst_lo += [jnp.where(low, t, 0.0).astype(BF16), jnp.where(low, r, 0.0).astype(BF16)]
            dst_hi += [jnp.where(low, 0.0, r).astype(BF16), jnp.where(low, 0.0, t).astype(BF16)]
    col = lax.broadcasted_iota(jnp.int32, (2 * cq, win), 1)
    top = lax.broadcasted_iota(jnp.int32, (2 * cq, 1), 0) < cq
    n_chunks = 1 if has_cache else tq // cq
    if n_chunks * cq < tq:
        att[...] = jnp.zeros((tq, D_MODEL), F32)

    def soft(sc, sink):
        m = jnp.maximum(jnp.max(sc, axis=-1, keepdims=True), sink)
        p = jnp.exp(sc - m)
        return p.astype(BF16), 1.0 / (jnp.sum(p, axis=-1, keepdims=True) + jnp.exp(sink - m))

    for c0 in range(0, n_chunks, SWA_STAGE_CHUNKS):
        units = [(c, kh) for c in range(c0, min(c0 + SWA_STAGE_CHUNKS, n_chunks)) for kh in range(SWA_KV_HEADS)]
        scores = []
        for c, kh in units:
            r0 = c * cq
            qs = jnp.concatenate([q[r0:r0 + cq, (2 * kh) * LANES:(2 * kh + 1) * LANES],
                                  q[r0:r0 + cq, (2 * kh + 1) * LANES:(2 * kh + 2) * LANES]], axis=0)
            keys = slice(r0, r0 + win)
            scores.append((_dot_nt(qs, k_lo[kh][keys]), _dot_nt(qs, k_hi[kh][keys])))
        probs = []
        for (c, kh), (s_e, s_o) in zip(units, scores):
            key_row = col + c * cq
            if has_cache:
                ok = key_row < KV_WIN + n_valid
            elif c * cq < KV_WIN:
                ok = jnp.logical_or(key_row >= KV_WIN, i > 0)
            else:
                ok = None
            if ok is not None:
                s_e, s_o = jnp.where(ok, s_e, -1e30), jnp.where(ok, s_o, -1e30)
            h0 = kh * SWA_GROUPS
            probs.append(soft(s_e, jnp.where(top, sink_ref[h0], sink_ref[h0 + 2]))
                         + soft(s_o, jnp.where(top, sink_ref[h0 + 1], sink_ref[h0 + 3])))
        for (c, kh), (p_e, inv_e, p_o, inv_o) in zip(units, probs):
            r0 = c * cq
            keys = slice(r0, r0 + win)
            o = inv_e * jnp.dot(p_e, v_lo[kh][keys], preferred_element_type=F32) \
                + inv_o * jnp.dot(p_o, v_hi[kh][keys], preferred_element_type=F32)
            att[r0:r0 + cq, (2 * kh) * LANES:(2 * kh + 1) * LANES] = o[:cq]
            att[r0:r0 + cq, (2 * kh + 1) * LANES:(2 * kh + 2) * LANES] = o[cq:]
    x3 = x + _dot(att[...], wo_ref[...])
    x4 = _swiglu(x3, nw_ref, wg_ref, wu_ref, wd_ref)
    o_ref[...] = _rms(x4, nwf_ref[...])


def _swa(sinks, x3d, kv3d, kv_prev, n_valid, has_cache, nwm, wq, wo, nw, wg, wu, wd, nwf, tq, cq):
    bsz, lp, _ = x3d.shape
    kern = functools.partial(_swa_kernel, tq=tq, cq=cq, n_valid=n_valid, has_cache=has_cache)
    per = tq // KV_WIN
    if has_cache:
        prev_map = lambda b, i, s: (b, 0, 0)
    else:
        prev_map = lambda b, i, s: (b, jnp.maximum(i * per - 1, 0), 0)
    grid_spec = pltpu.PrefetchScalarGridSpec(
        num_scalar_prefetch=1,
        grid=(bsz, lp // tq),
        in_specs=[pl.BlockSpec((None, tq, D_MODEL), lambda b, i, s: (b, i, 0)),
                  pl.BlockSpec((None, tq, KV_COLS), lambda b, i, s: (b, i, 0)),
                  pl.BlockSpec((None, KV_WIN, KV_COLS), prev_map),
                  _const_spec((1, D_MODEL)), _const_spec((D_MODEL, D_MODEL)), _const_spec((D_MODEL, D_MODEL))]
                 + _ffn_specs() + [_const_spec((1, D_MODEL))],
        out_specs=pl.BlockSpec((None, tq, D_MODEL), lambda b, i, s: (b, i, 0)),
        scratch_shapes=[pltpu.VMEM((KV_WIN + tq, KV_COLS), F32), pltpu.VMEM((tq, D_MODEL), F32)],
    )
    return pl.pallas_call(
        kern,
        out_shape=jax.ShapeDtypeStruct((bsz, lp, D_MODEL), F32),
        grid_spec=grid_spec,
        compiler_params=pltpu.CompilerParams(dimension_semantics=("arbitrary", "arbitrary"),
                                             vmem_limit_bytes=VMEM_LIMIT),
        name="swa_ffn1",
    )(sinks, x3d, kv3d, kv_prev, nwm, wq, wo, nw, wg, wu, wd, nwf)


def _prep_params(p):
    q = {}
    w_in = p["w_in"][0]
    off_qkv = S5_WIDTH
    off_z = off_qkv + GDN_CONV_CH
    off_b = off_z + GDN_V
    w_perm = jnp.concatenate(
        [w_in[:, off_qkv:off_z], w_in[:, :S5_WIDTH], w_in[:, off_z:off_b], w_in[:, off_b:],
         jnp.zeros((D_MODEL, LANES - 2 * GDN_HEADS), F32)], axis=1)
    q["w_in"] = w_perm.astype(BF16)

    lam_re, lam_im = p["s5_lam_re"][0], p["s5_lam_im"][0]
    dt = jnp.exp(p["s5_log_dt"][0])[:, None]
    mag = jnp.exp(lam_re * dt)
    lbr, lbi = mag * jnp.cos(lam_im * dt), mag * jnp.sin(lam_im * dt)
    den = lam_re * lam_re + lam_im * lam_im
    nr, ni = lbr - 1.0, lbi
    cr = (nr * lam_re + ni * lam_im) / den
    ci = (ni * lam_re - nr * lam_im) / den
    b_re, b_im = p["s5_b_re"][0], p["s5_b_im"][0]
    bbr = cr[..., None] * b_re - ci[..., None] * b_im
    bbi = cr[..., None] * b_im + ci[..., None] * b_re
    gb = S5_GROUPS // S5_BLOCKS
    eye = jnp.eye(gb, dtype=F32)

    def in_blocks(m):
        m = m.reshape(S5_BLOCKS, gb, S5_P, S5_GROUP)
        return jnp.einsum("jgpc,gh->jgchp", m, eye).reshape(S5_BLOCKS, gb * S5_GROUP, gb * S5_P)

    def out_blocks(m):
        m = m.reshape(S5_BLOCKS, gb, S5_GROUP, S5_P)
        return jnp.einsum("jgcp,gh->jgphc", m, eye).reshape(S5_BLOCKS, gb * S5_P, gb * S5_GROUP)

    q["s5_wb"] = jnp.concatenate([in_blocks(bbr), in_blocks(bbi)], axis=2).astype(BF16)
    q["s5_wc"] = jnp.concatenate([out_blocks(p["s5_c_re"][0]), -out_blocks(p["s5_c_im"][0])], axis=1).astype(BF16)
    q["s5_lamr"] = lbr.reshape(1, S5_STATE)
    q["s5_lami"] = lbi.reshape(1, S5_STATE)
    q["s5_d"] = p["s5_d"][0].reshape(1, S5_WIDTH)
    q["s5_wglu"] = p["s5_w_glu"][0].astype(BF16)
    q["s5_bglu"] = p["s5_b_glu"][0].reshape(1, S5_WIDTH)

    def head_row(v):
        return jnp.zeros((1, LANES), F32).at[0, GDN_HEADS:2 * GDN_HEADS].set(v)

    q["gdn_alog"] = head_row(p["gdn_a_log"][0])
    q["gdn_dtb"] = head_row(p["gdn_dt_bias"][0])
    q["gdn_conv_w"] = p["gdn_conv_w"][0]
    q["gdn_nw"] = p["gdn_norm_w"][0].reshape(1, GDN_DV)
    q["w_out"] = p["w_out_ab"][0].astype(BF16)
    q["wq"] = p["swa_wq"][0].astype(BF16)
    q["wkv"] = jnp.concatenate([p["swa_wk"][0], p["swa_wv"][0]], axis=1).astype(BF16)
    q["wo"] = p["swa_wo"][0].astype(BF16)
    q["sinks"] = p["swa_sinks"][0]
    for name in ("norm_mix", "norm_ffn"):
        q[name] = p[name].reshape(-1, 1, D_MODEL)
    q["norm_final"] = p["norm_final"].reshape(1, D_MODEL)
    for name in ("ffn_w_gate", "ffn_w_up", "ffn_w_down"):
        q[name] = p[name].astype(BF16)
    return q


def _tile(n, target):
    t = min(n, target)
    while n % t:
        t //= 2
    return t


def _trunk(x, s5_re, s5_im, gdn_s, gdn_conv, cache_k, cache_v, q, n_valid, tiles):
    bsz, lp, _ = x.shape
    has_cache = cache_k is not None
    t = bsz * lp
    tm = _tile(t, tiles["tm"])
    x2d = x.reshape(t, D_MODEL)

    proj, qkv_tail = _in_proj(x, n_valid, q["norm_mix"][0], q["w_in"], q["gdn_conv_w"], gdn_conv, min(tm, lp))
    proj = proj.reshape(bsz, lp, PROJ_COLS)
    a_out, x_re, x_im = _s5(proj, n_valid, s5_re.reshape(bsz, S5_STATE), s5_im.reshape(bsz, S5_STATE),
                            q["s5_lamr"], q["s5_lami"], q["s5_wb"], q["s5_wc"], q["s5_d"], q["s5_wglu"],
                            q["s5_bglu"], _tile(n_valid, tiles["tc"]))
    if n_valid != lp:
        a_out = jnp.pad(a_out, ((0, 0), (0, lp - n_valid), (0, 0)))
    b_out, s_fin = _gdn(proj, n_valid, gdn_s, q["gdn_alog"], q["gdn_dtb"], q["gdn_nw"], _tile(lp, tiles["tt"]))
    new_buf = qkv_tail[:, -(GDN_CONV - 1):]

    x2, kv = _mix0(x2d, a_out.reshape(t, S5_WIDTH), b_out.reshape(t, GDN_V), q["w_out"], q["norm_ffn"][0],
                   q["ffn_w_gate"][0], q["ffn_w_up"][0], q["ffn_w_down"][0], q["norm_mix"][1], q["wkv"], tm)
    kv = kv.reshape(bsz, lp, KV_COLS)
    half = SWA_KV_HEADS * SWA_HD
    if has_cache:
        kv_prev = jnp.concatenate([cache_k.reshape(bsz, KV_WIN, half), cache_v.reshape(bsz, KV_WIN, half)], axis=-1)
        tq, cq = lp, n_valid
    else:
        assert n_valid == lp
        kv_prev = kv
        tq, cq = _tile(lp, tiles["tq"]), CHUNK
    y = _swa(q["sinks"], x2.reshape(bsz, lp, D_MODEL), kv, kv_prev, n_valid, has_cache, q["norm_mix"][1],
             q["wq"], q["wo"], q["norm_ffn"][1], q["ffn_w_gate"][1], q["ffn_w_up"][1], q["ffn_w_down"][1],
             q["norm_final"], tq, cq)

    k_new = kv[:, :n_valid, :half]
    v_new = kv[:, :n_valid, half:]
    if has_cache:
        k_new = jnp.concatenate([cache_k.reshape(bsz, KV_WIN, half), k_new], axis=1)
        v_new = jnp.concatenate([cache_v.reshape(bsz, KV_WIN, half), v_new], axis=1)
    ck = k_new[:, -KV_WIN:].reshape(bsz, KV_WIN, SWA_KV_HEADS, SWA_HD)
    cv = v_new[:, -KV_WIN:].reshape(bsz, KV_WIN, SWA_KV_HEADS, SWA_HD)
    return (y[:, :n_valid], x_re.reshape(1, bsz, S5_GROUPS, S5_P), x_im.reshape(1, bsz, S5_GROUPS, S5_P),
            s_fin[None], new_buf[None], ck[None], cv[None])


PROMPT_TILES = dict(tm=512, tc=64, tt=256, tq=512)
SAMPLE_PAD = 64


def kernel(x_prompt, x_sample, state_s5_re, state_s5_im, state_gdn, state_gdn_conv, cache_swa_k, cache_swa_v,
           norm_mix, norm_ffn, norm_final, w_in, s5_lam_re, s5_lam_im, s5_log_dt, s5_b_re, s5_b_im,
           s5_c_re, s5_c_im, s5_d, s5_w_glu, s5_b_glu, gdn_conv_w, gdn_a_log, gdn_dt_bias, gdn_norm_w,
           w_out_ab, swa_wq, swa_wk, swa_wv, swa_sinks, swa_wo, ffn_w_gate, ffn_w_up, ffn_w_down):
    p = dict(norm_mix=norm_mix, norm_ffn=norm_ffn, norm_final=norm_final, w_in=w_in,
             s5_lam_re=s5_lam_re, s5_lam_im=s5_lam_im, s5_log_dt=s5_log_dt, s5_b_re=s5_b_re, s5_b_im=s5_b_im,
             s5_c_re=s5_c_re, s5_c_im=s5_c_im, s5_d=s5_d, s5_w_glu=s5_w_glu, s5_b_glu=s5_b_glu,
             gdn_conv_w=gdn_conv_w, gdn_a_log=gdn_a_log, gdn_dt_bias=gdn_dt_bias, gdn_norm_w=gdn_norm_w,
             w_out_ab=w_out_ab, swa_wq=swa_wq, swa_wk=swa_wk, swa_wv=swa_wv, swa_sinks=swa_sinks,
             swa_wo=swa_wo, ffn_w_gate=ffn_w_gate, ffn_w_up=ffn_w_up, ffn_w_down=ffn_w_down)
    q = _prep_params(p)
    bsz, seq, _ = x_prompt.shape
    dbsz, dseq, _ = x_sample.shape
    zeros_s5 = jnp.zeros((bsz, S5_GROUPS, S5_P), F32)
    outs_p = _trunk(x_prompt, zeros_s5, zeros_s5, jnp.zeros((bsz, GDN_HEADS, GDN_DK, GDN_DV), F32),
                    jnp.zeros((bsz, GDN_CONV - 1, GDN_CONV_CH), F32), None, None, q, seq, PROMPT_TILES)
    x_s = jnp.pad(x_sample, ((0, 0), (0, SAMPLE_PAD - dseq), (0, 0)))
    outs_s = _trunk(x_s, state_s5_re[0], state_s5_im[0], state_gdn[0], state_gdn_conv[0],
                    cache_swa_k[0], cache_swa_v[0], q, dseq, PROMPT_TILES)
    return (outs_p[0], outs_s[0]) + tuple(outs_p[1:]) + tuple(outs_s[1:])
```

```python
import functools
import math

import jax
import jax.numpy as jnp
from jax import lax
from jax.experimental import pallas as pl
from jax.experimental.pallas import tpu as pltpu

F32 = jnp.float32
BF16 = jnp.bfloat16

D_MODEL = 1024
CHUNK = 64
RMS_EPS = 1e-6
L2_EPS = 1e-6

S5_WIDTH = 512
S5_GROUP = 16
S5_GROUPS = 32
S5_P = 64
S5_STATE = S5_GROUPS * S5_P
S5_BLOCKS = 4
S5_BLOCK_STATE = S5_STATE // S5_BLOCKS

GDN_HEADS = 4
GDN_DK = 128
GDN_DV = 128
GDN_QK = GDN_HEADS * GDN_DK
GDN_V = GDN_HEADS * GDN_DV
GDN_CONV = 4
GDN_CONV_CH = 2 * GDN_QK + GDN_V
SOLVE_SPLIT_WIDTH = 8
GDN_ROWS_PER_STEP = 4
TAIL_ROWS = 8

SWA_HEADS = 16
SWA_KV_HEADS = 4
SWA_GROUPS = 4
SWA_HD = 64
KV_WIN = 128
KV_COLS = 2 * SWA_KV_HEADS * SWA_HD
SWA_ROWS_PER_STEP = 256
SWA_STAGE_CHUNKS = 2

D_FF = 2816
FF_CHUNK = 256

PROJ_QKV = 0
PROJ_U = GDN_CONV_CH
PROJ_Z = PROJ_U + S5_WIDTH
PROJ_BA = PROJ_Z + GDN_V
PROJ_COLS = PROJ_BA + 128

LANES = 128
VMEM_LIMIT = 56 * 1024 * 1024


def _rms(x, w):
    return x * lax.rsqrt(jnp.mean(x * x, axis=-1, keepdims=True) + RMS_EPS) * w


def _dot(a, b):
    return jnp.dot(a.astype(BF16), b.astype(BF16), preferred_element_type=F32)


def _dot_nt(a, b):
    return lax.dot_general(a.astype(BF16), b.astype(BF16), (((1,), (1,)), ((), ())),
                           preferred_element_type=F32)


def _const_spec(shape):
    return pl.BlockSpec(shape, lambda *_: (0,) * len(shape), pipeline_mode=pl.Buffered(1))


def _in_proj_kernel(x_ref, nw_ref, w_ref, cw_ref, buf_ref, o_ref, tail_ref, xp, *, tm, tiles_per_seq, tail_tile,
                    tail_off):
    it = pl.program_id(0) % tiles_per_seq
    pad = TAIL_ROWS

    @pl.when(it == 0)
    def _():
        xp[0:pad, :] = jnp.zeros((pad, GDN_CONV_CH), F32)
        xp[pad - (GDN_CONV - 1):pad, :] = buf_ref[...]

    p = _dot(_rms(x_ref[...], nw_ref[...]), w_ref[...])
    o_ref[:, GDN_CONV_CH:] = p[:, GDN_CONV_CH:]
    xp[pad:pad + tm, :] = p[:, :GDN_CONV_CH]

    y = xp[pad:pad + tm, :] * cw_ref[GDN_CONV - 1:GDN_CONV, :]
    for j in range(GDN_CONV - 1):
        off = pad - (GDN_CONV - 1) + j
        y = y + xp[off:off + tm, :] * cw_ref[j:j + 1, :]
    xp[0:pad, :] = xp[tm:tm + pad, :]
    y = jax.nn.silu(y)
    for h in range(2 * GDN_HEADS):
        cols = slice(h * GDN_DK, (h + 1) * GDN_DK)
        x_h = y[:, cols]
        scale = GDN_DK ** -0.5 if h < GDN_HEADS else 1.0
        o_ref[:, cols] = x_h * (lax.rsqrt(jnp.sum(x_h * x_h, axis=-1, keepdims=True) + L2_EPS) * scale)
    o_ref[:, 2 * GDN_QK:GDN_CONV_CH] = y[:, 2 * GDN_QK:]

    @pl.when(it == tail_tile)
    def _():
        tail_ref[...] = xp[pad + tail_off:pad + tail_off + TAIL_ROWS, :]


def _in_proj(x3d, n_valid, norm_w, w_perm, conv_w, conv_buf, tm):
    bsz, lp, _ = x3d.shape
    t = bsz * lp
    tiles_per_seq = lp // tm
    assert n_valid >= TAIL_ROWS and (n_valid - TAIL_ROWS) // tm == (n_valid - 1) // tm
    tail_tile = (n_valid - 1) // tm
    kern = functools.partial(_in_proj_kernel, tm=tm, tiles_per_seq=tiles_per_seq, tail_tile=tail_tile,
                             tail_off=n_valid - TAIL_ROWS - tail_tile * tm)
    return pl.pallas_call(
        kern,
        out_shape=(jax.ShapeDtypeStruct((t, PROJ_COLS), F32),
                   jax.ShapeDtypeStruct((bsz, TAIL_ROWS, GDN_CONV_CH), F32)),
        grid=(t // tm,),
        in_specs=[pl.BlockSpec((tm, D_MODEL), lambda i: (i, 0)),
                  _const_spec((1, D_MODEL)),
                  _const_spec((D_MODEL, PROJ_COLS)),
                  _const_spec((GDN_CONV, GDN_CONV_CH)),
                  pl.BlockSpec((None, GDN_CONV - 1, GDN_CONV_CH), lambda i: (i // tiles_per_seq, 0, 0))],
        out_specs=(pl.BlockSpec((tm, PROJ_COLS), lambda i: (i, 0)),
                   pl.BlockSpec((None, TAIL_ROWS, GDN_CONV_CH), lambda i: (i // tiles_per_seq, 0, 0))),
        scratch_shapes=[pltpu.VMEM((tm + TAIL_ROWS, GDN_CONV_CH), F32)],
        compiler_params=pltpu.CompilerParams(dimension_semantics=("arbitrary",),
                                             vmem_limit_bytes=VMEM_LIMIT),
        name="in_proj",
    )(x3d.reshape(t, D_MODEL), norm_w, w_perm, conv_w, conv_buf)


def _s5_kernel(u_ref, x0r_ref, x0i_ref, lamr_ref, lami_ref, wb_ref, wc_ref, d_ref, wglu_ref, bglu_ref,
               o_ref, xr_out, xi_out, st_r, st_i, xs, *, tc, bsz):
    i = pl.program_id(0)

    @pl.when(i == 0)
    def _():
        st_r[...] = x0r_ref[...]
        st_i[...] = x0i_ref[...]

    u_tm = jnp.swapaxes(u_ref[...], 0, 1).reshape(tc * bsz, S5_WIDTH)
    half = S5_BLOCK_STATE
    ys = []
    for j in range(S5_BLOCKS):
        bu = _dot(u_tm[:, j * LANES:(j + 1) * LANES], wb_ref[j])
        xs[j] = bu.reshape(tc, bsz, 2 * half)
        lr = jnp.broadcast_to(lamr_ref[:, j * half:(j + 1) * half], (bsz, half))
        li = jnp.broadcast_to(lami_ref[:, j * half:(j + 1) * half], (bsz, half))
        xr = st_r[:, j * half:(j + 1) * half]
        xi = st_i[:, j * half:(j + 1) * half]
        for t in range(tc):
            xr, xi = (lr * xr - li * xi + xs[j, t, :, :half], lr * xi + li * xr + xs[j, t, :, half:])
            xs[j, t, :, :half] = xr
            xs[j, t, :, half:] = xi
        st_r[:, j * half:(j + 1) * half] = xr
        st_i[:, j * half:(j + 1) * half] = xi
        ys.append(_dot(xs[j].reshape(tc * bsz, 2 * half), wc_ref[j]))
    y = jnp.concatenate(ys, axis=-1) + d_ref[...] * u_tm
    z = jax.nn.gelu(y)
    out = z * jax.nn.sigmoid(_dot(z, wglu_ref[...]) + bglu_ref[...])
    o_ref[...] = jnp.swapaxes(out.reshape(tc, bsz, S5_WIDTH), 0, 1)
    xr_out[...] = st_r[...]
    xi_out[...] = st_i[...]


def _s5(proj3, n_valid, x0r, x0i, lamr, lami, wb, wc, d_row, wglu, bglu_row, tc):
    bsz = proj3.shape[0]
    kern = functools.partial(_s5_kernel, tc=tc, bsz=bsz)
    return pl.pallas_call(
        kern,
        out_shape=(jax.ShapeDtypeStruct((bsz, n_valid, S5_WIDTH), F32),
                   jax.ShapeDtypeStruct((bsz, S5_STATE), F32),
                   jax.ShapeDtypeStruct((bsz, S5_STATE), F32)),
        grid=(n_valid // tc,),
        in_specs=[pl.BlockSpec((bsz, tc, S5_WIDTH), lambda i: (0, i, PROJ_U // S5_WIDTH)),
                  _const_spec((bsz, S5_STATE)), _const_spec((bsz, S5_STATE)),
                  _const_spec((1, S5_STATE)), _const_spec((1, S5_STATE)),
                  _const_spec((S5_BLOCKS, LANES, 2 * S5_BLOCK_STATE)),
                  _const_spec((S5_BLOCKS, 2 * S5_BLOCK_STATE, LANES)),
                  _const_spec((1, S5_WIDTH)), _const_spec((S5_WIDTH, S5_WIDTH)), _const_spec((1, S5_WIDTH))],
        out_specs=(pl.BlockSpec((bsz, tc, S5_WIDTH), lambda i: (0, i, 0)),
                   pl.BlockSpec((bsz, S5_STATE), lambda i: (0, 0)),
                   pl.BlockSpec((bsz, S5_STATE), lambda i: (0, 0))),
        scratch_shapes=[pltpu.VMEM((bsz, S5_STATE), F32), pltpu.VMEM((bsz, S5_STATE), F32),
                        pltpu.VMEM((S5_BLOCKS, tc, bsz, 2 * S5_BLOCK_STATE), F32)],
        compiler_params=pltpu.CompilerParams(dimension_semantics=("arbitrary",),
                                             vmem_limit_bytes=VMEM_LIMIT),
        name="s5_mixer",
    )(proj3, x0r, x0i, lamr, lami, wb, wc, d_row, wglu, bglu_row)


def _solve_unit_lower(a_strict, rhs, block):
    n = len(rhs)
    w = rhs[0].shape[1]

    def hi_lo(x):
        hi = x.astype(BF16)
        return [hi, (x - hi.astype(F32)).astype(BF16)]

    m = [(-a).astype(BF16) for a in a_strict]
    x = list(rhs)
    width = 1
    while width < block:
        last = 2 * width >= block
        if width < SOLVE_SPLIT_WIDTH:
            p = [jnp.dot(m[j], jnp.concatenate(hi_lo(x[j]) + ([] if last else [m[j]]), axis=1),
                         preferred_element_type=F32) for j in range(n)]
            x = [x[j] + (p[j][:, :w] + p[j][:, w:2 * w]) for j in range(n)]
            if not last:
                m = [p[j][:, 2 * w:].astype(BF16) for j in range(n)]
        else:
            x = [x[j] + jnp.dot(m[j], x[j].astype(BF16), preferred_element_type=F32) for j in range(n)]
            if not last:
                m = [jnp.dot(m[j], m[j], preferred_element_type=F32).astype(BF16) for j in range(n)]
        width *= 2
    return x


def _gdn_kernel(qkv_ref, z_ref, ba_ref, s0_ref, alog_ref, dtb_ref, nw_ref, o_ref, s_out, s_st, *, tt, n_valid,
                rows_per_step):
    i = pl.program_id(1)
    nc = tt // CHUNK

    @pl.when(i == 0)
    def _():
        s_st[...] = s0_ref[...]

    ri = lax.broadcasted_iota(jnp.int32, (tt, tt), 0)
    ci = lax.broadcasted_iota(jnp.int32, (tt, tt), 1)
    same = (ri // CHUNK) == (ci // CHUNK)
    incl = jnp.logical_and(same, ri >= ci)
    strict = jnp.logical_and(same, ri > ci)
    row = lax.broadcasted_iota(jnp.int32, (tt, LANES), 0)
    pos = row % CHUNK

    units = [(r, h) for r in range(rows_per_step) for h in range(GDN_HEADS)]
    gates = []
    for r in range(rows_per_step):
        ba = ba_ref[r]
        beta_all = jax.nn.sigmoid(ba)
        g_all = -jnp.exp(alog_ref[...]) * jax.nn.softplus(ba + dtb_ref[...])
        if n_valid % tt != 0:
            live = (i * tt + row) < n_valid
            beta_all = jnp.where(live, beta_all, 0.0)
            g_all = jnp.where(live, g_all, 0.0)
        gc_all = g_all
        shift = 1
        while shift < CHUNK:
            gc_all = gc_all + jnp.where(pos >= shift, pltpu.roll(gc_all, shift, axis=0), 0.0)
            shift *= 2
        g_end_all = jnp.concatenate(
            [jnp.broadcast_to(gc_all[c * CHUNK + CHUNK - 1:(c + 1) * CHUNK, :], (CHUNK, LANES)) for c in range(nc)],
            axis=0)
        gates.append((beta_all, gc_all, g_end_all, gc_all.T, (g_end_all - gc_all).T))

    kq, kb, kt, vb, qn = [], [], [], [], []
    for r, h in units:
        q_h = qkv_ref[r, :, h * GDN_DK:(h + 1) * GDN_DK]
        k_h = qkv_ref[r, :, GDN_QK + h * GDN_DK:GDN_QK + (h + 1) * GDN_DK]
        v_h = qkv_ref[r, :, 2 * GDN_QK + h * GDN_DV:2 * GDN_QK + (h + 1) * GDN_DV]
        beta_b = jnp.broadcast_to(gates[r][0][:, h:h + 1], (tt, GDN_DK))
        kb.append(k_h * beta_b)
        vb.append(v_h * beta_b)
        kt.append(k_h.T)
        qn.append(q_h)
        kq.append(_dot(jnp.concatenate([kb[-1], q_h], axis=0), kt[-1]))
    a_mat, attn, rhs, q_dec, kdt = [], [], [], [], []
    for n, (r, h) in enumerate(units):
        _, gc_all, _, gc_t, rest_t = gates[r]
        gc_b = jnp.broadcast_to(gc_all[:, GDN_HEADS + h:GDN_HEADS + h + 1], (tt, max(tt, GDN_DK)))
        gexp_b = jnp.exp(gc_b[:, :GDN_DK])
        grow = gc_t[GDN_HEADS + h:GDN_HEADS + h + 1, :]
        decay = jnp.where(incl, jnp.exp(jnp.where(incl, gc_b[:, :tt] - grow, 0.0)), 0.0)
        a_mat.append(jnp.where(strict, kq[n][:tt] * decay, 0.0))
        attn.append(kq[n][tt:] * decay)
        rhs.append(jnp.concatenate([vb[n], kb[n] * gexp_b], axis=1))
        q_dec.append(qn[n] * gexp_b)
        kdt.append(kt[n] * jnp.exp(rest_t[GDN_HEADS + h:GDN_HEADS + h + 1, :]))
    uw = _solve_unit_lower(a_mat, rhs, CHUNK)
    u_val = [x[:, :GDN_DV] for x in uw]
    w_key = [x[:, GDN_DV:] for x in uw]

    s = [s_st[r, h] for r, h in units]
    o_state = [[] for _ in units]
    v_new = [[] for _ in units]
    for c in range(nc):
        rows = slice(c * CHUNK, (c + 1) * CHUNK)
        wq = [_dot(jnp.concatenate([w_key[n][rows], q_dec[n][rows]], axis=0), s[n]) for n in range(len(units))]
        for n in range(len(units)):
            o_state[n].append(wq[n][CHUNK:])
            v_new[n].append(u_val[n][rows] - wq[n][:CHUNK])
        upd = [_dot(kdt[n][:, rows], v_new[n][c]) for n in range(len(units))]
        for n, (r, h) in enumerate(units):
            g_last = jnp.exp(gates[r][2][c * CHUNK:c * CHUNK + 1, GDN_HEADS + h:GDN_HEADS + h + 1])
            s[n] = s[n] * g_last + upd[n]
    o_all = [jnp.concatenate(o_state[n], axis=0) + _dot(attn[n], jnp.concatenate(v_new[n], axis=0))
             for n in range(len(units))]
    for n, (r, h) in enumerate(units):
        s_st[r, h] = s[n]
        s_out[r, h] = s[n]
        o = _rms(o_all[n], nw_ref[...])
        o_ref[r, :, h * GDN_DV:(h + 1) * GDN_DV] = o * jax.nn.silu(z_ref[r, :, h * GDN_DV:(h + 1) * GDN_DV])


def _gdn(proj3, n_valid, s0, alog_row, dtb_row, nw_row, tt):
    bsz, lp, _ = proj3.shape
    rps = GDN_ROWS_PER_STEP
    kern = functools.partial(_gdn_kernel, tt=tt, n_valid=n_valid, rows_per_step=rps)
    return pl.pallas_call(
        kern,
        out_shape=(jax.ShapeDtypeStruct((bsz, lp, GDN_V), F32),
                   jax.ShapeDtypeStruct((bsz, GDN_HEADS, GDN_DK, GDN_DV), F32)),
        grid=(bsz // rps, lp // tt),
        in_specs=[pl.BlockSpec((rps, tt, GDN_CONV_CH), lambda b, i: (b, i, PROJ_QKV // GDN_CONV_CH)),
                  pl.BlockSpec((rps, tt, GDN_V), lambda b, i: (b, i, PROJ_Z // GDN_V)),
                  pl.BlockSpec((rps, tt, LANES), lambda b, i: (b, i, PROJ_BA // LANES)),
                  pl.BlockSpec((rps, GDN_HEADS, GDN_DK, GDN_DV), lambda b, i: (b, 0, 0, 0)),
                  _const_spec((1, LANES)), _const_spec((1, LANES)), _const_spec((1, GDN_DV))],
        out_specs=(pl.BlockSpec((rps, tt, GDN_V), lambda b, i: (b, i, 0)),
                   pl.BlockSpec((rps, GDN_HEADS, GDN_DK, GDN_DV), lambda b, i: (b, 0, 0, 0))),
        scratch_shapes=[pltpu.VMEM((rps, GDN_HEADS, GDN_DK, GDN_DV), F32)],
        compiler_params=pltpu.CompilerParams(dimension_semantics=("arbitrary", "arbitrary"),
                                             vmem_limit_bytes=VMEM_LIMIT),
        name="gdn_mixer",
    )(proj3, proj3, proj3, s0, alog_row, dtb_row, nw_row)


def _swiglu(x, nw_ref, wg_ref, wu_ref, wd_ref):
    h = _rms(x, nw_ref[...]).astype(BF16)
    acts = []
    for f in range(0, D_FF, FF_CHUNK):
        g = jnp.dot(h, wg_ref[:, f:f + FF_CHUNK], preferred_element_type=F32)
        u = jnp.dot(h, wu_ref[:, f:f + FF_CHUNK], preferred_element_type=F32)
        acts.append((jax.nn.silu(g) * u).astype(BF16))
    return x + jnp.dot(jnp.concatenate(acts, axis=1), wd_ref[...], preferred_element_type=F32)


def _ffn_specs():
    return [_const_spec((1, D_MODEL)), _const_spec((D_MODEL, D_FF)), _const_spec((D_MODEL, D_FF)),
            _const_spec((D_FF, D_MODEL))]


def _mix0_kernel(x_ref, a_ref, b_ref, wo_ref, nw_ref, wg_ref, wu_ref, wd_ref, nw1_ref, wkv_ref,
                 o_ref, kv_ref):
    mix = _dot(a_ref[...], wo_ref[:S5_WIDTH, :]) + _dot(b_ref[...], wo_ref[S5_WIDTH:, :])
    x2 = _swiglu(x_ref[...] + mix, nw_ref, wg_ref, wu_ref, wd_ref)
    o_ref[...] = x2
    kv_ref[...] = _dot(_rms(x2, nw1_ref[...]), wkv_ref[...])


def _mix0(x2d, a2d, b2d, wo, nw, wg, wu, wd, nw1, wkv, tm):
    t = x2d.shape[0]
    return pl.pallas_call(
        _mix0_kernel,
        out_shape=(jax.ShapeDtypeStruct((t, D_MODEL), F32), jax.ShapeDtypeStruct((t, KV_COLS), F32)),
        grid=(t // tm,),
        in_specs=[pl.BlockSpec((tm, D_MODEL), lambda i: (i, 0)),
                  pl.BlockSpec((tm, S5_WIDTH), lambda i: (i, 0)),
                  pl.BlockSpec((tm, GDN_V), lambda i: (i, 0)),
                  _const_spec((S5_WIDTH + GDN_V, D_MODEL))] + _ffn_specs() +
                 [_const_spec((1, D_MODEL)), _const_spec((D_MODEL, KV_COLS))],
        out_specs=(pl.BlockSpec((tm, D_MODEL), lambda i: (i, 0)),
                   pl.BlockSpec((tm, KV_COLS), lambda i: (i, 0))),
        compiler_params=pltpu.CompilerParams(dimension_semantics=("arbitrary",),
                                             vmem_limit_bytes=VMEM_LIMIT),
        name="mix0_ffn0",
    )(x2d, a2d, b2d, wo, nw, wg, wu, wd, nw1, wkv)


def _swa_kernel(sink_ref, x_ref, kv_ref, kvp_ref, nwm_ref, wq_ref, wo_ref, nw_ref, wg_ref, wu_ref, wd_ref,
                nwf_ref, o_ref, kvc, att, *, tq, cq, n_valid, has_cache, seqs):
    i = pl.program_id(1)
    x = x_ref[...].reshape(seqs * tq, D_MODEL)
    q = (_dot(_rms(x, nwm_ref[...]), wq_ref[...]) * (SWA_HD ** -0.5)).astype(BF16)
    win = KV_WIN + cq
    rows_all = KV_WIN + tq
    low = lax.broadcasted_iota(jnp.int32, (rows_all, LANES), 1) < SWA_HD
    col = lax.broadcasted_iota(jnp.int32, (2 * cq, win), 1)
    top = lax.broadcasted_iota(jnp.int32, (2 * cq, 1), 0) < cq
    n_chunks = 1 if has_cache else tq // cq
    if n_chunks * cq < tq:
        att[...] = jnp.zeros((seqs * tq, D_MODEL), F32)

    def soft(sc, sink):
        m = jnp.maximum(jnp.max(sc, axis=-1, keepdims=True), sink)
        p = jnp.exp(sc - m)
        return p.astype(BF16), 1.0 / (jnp.sum(p, axis=-1, keepdims=True) + jnp.exp(sink - m))

    for sq in range(seqs):
        kvc[sq, 0:KV_WIN, :] = kvp_ref[sq]
        kvc[sq, KV_WIN:KV_WIN + tq, :] = kv_ref[sq]
        k_lo, k_hi, v_lo, v_hi = [], [], [], []
        for dst_lo, dst_hi, base in ((k_lo, k_hi, 0), (v_lo, v_hi, SWA_KV_HEADS * SWA_HD)):
            for j in range(SWA_KV_HEADS // 2):
                t = kvc[sq, :, base + j * LANES:base + (j + 1) * LANES]
                r = pltpu.roll(t, SWA_HD, axis=1)
                dst_lo += [jnp.where(low, t, 0.0).astype(BF16), jnp.where(low, r, 0.0).astype(BF16)]
                dst_hi += [jnp.where(low, 0.0, r).astype(BF16), jnp.where(low, 0.0, t).astype(BF16)]
        for c0 in range(0, n_chunks, SWA_STAGE_CHUNKS):
            units = [(c, kh) for c in range(c0, min(c0 + SWA_STAGE_CHUNKS, n_chunks)) for kh in range(SWA_KV_HEADS)]
            scores = []
            for c, kh in units:
                r0 = sq * tq + c * cq
                qs = jnp.concatenate([q[r0:r0 + cq, (2 * kh) * LANES:(2 * kh + 1) * LANES],
                                      q[r0:r0 + cq, (2 * kh + 1) * LANES:(2 * kh + 2) * LANES]], axis=0)
                keys = slice(c * cq, c * cq + win)
                scores.append((_dot_nt(qs, k_lo[kh][keys]), _dot_nt(qs, k_hi[kh][keys])))
            probs = []
            for (c, kh), (s_e, s_o) in zip(units, scores):
                key_row = col + c * cq
                if has_cache:
                    ok = key_row < KV_WIN + n_valid
                elif c * cq < KV_WIN:
                    ok = jnp.logical_or(key_row >= KV_WIN, i > 0)
                else:
                    ok = None
                if ok is not None:
                    s_e, s_o = jnp.where(ok, s_e, -1e30), jnp.where(ok, s_o, -1e30)
                h0 = kh * SWA_GROUPS
                probs.append(soft(s_e, jnp.where(top, sink_ref[h0], sink_ref[h0 + 2]))
                             + soft(s_o, jnp.where(top, sink_ref[h0 + 1], sink_ref[h0 + 3])))
            for (c, kh), (p_e, inv_e, p_o, inv_o) in zip(units, probs):
                r0 = sq * tq + c * cq
                keys = slice(c * cq, c * cq + win)
                o = inv_e * jnp.dot(p_e, v_lo[kh][keys], preferred_element_type=F32) \
                    + inv_o * jnp.dot(p_o, v_hi[kh][keys], preferred_element_type=F32)
                att[r0:r0 + cq, (2 * kh) * LANES:(2 * kh + 1) * LANES] = o[:cq]
                att[r0:r0 + cq, (2 * kh + 1) * LANES:(2 * kh + 2) * LANES] = o[cq:]
    x3 = x + _dot(att[...], wo_ref[...])
    x4 = _swiglu(x3, nw_ref, wg_ref, wu_ref, wd_ref)
    o_ref[...] = _rms(x4, nwf_ref[...]).reshape(seqs, tq, D_MODEL)


def _swa(sinks, x3d, kv3d, kv_prev, n_valid, has_cache, nwm, wq, wo, nw, wg, wu, wd, nwf, tq, cq):
    bsz, lp, _ = x3d.shape
    seqs = max(1, min(bsz, SWA_ROWS_PER_STEP // tq)) if has_cache else 1
    assert bsz % seqs == 0
    kern = functools.partial(_swa_kernel, tq=tq, cq=cq, n_valid=n_valid, has_cache=has_cache, seqs=seqs)
    per = tq // KV_WIN
    if has_cache:
        prev_map = lambda b, i, s: (b, 0, 0)
    else:
        prev_map = lambda b, i, s: (b, jnp.maximum(i * per - 1, 0), 0)
    grid_spec = pltpu.PrefetchScalarGridSpec(
        num_scalar_prefetch=1,
        grid=(bsz // seqs, lp // tq),
        in_specs=[pl.BlockSpec((seqs, tq, D_MODEL), lambda b, i, s: (b, i, 0)),
                  pl.BlockSpec((seqs, tq, KV_COLS), lambda b, i, s: (b, i, 0)),
                  pl.BlockSpec((seqs, KV_WIN, KV_COLS), prev_map),
                  _const_spec((1, D_MODEL)), _const_spec((D_MODEL, D_MODEL)), _const_spec((D_MODEL, D_MODEL))]
                 + _ffn_specs() + [_const_spec((1, D_MODEL))],
        out_specs=pl.BlockSpec((seqs, tq, D_MODEL), lambda b, i, s: (b, i, 0)),
        scratch_shapes=[pltpu.VMEM((seqs, KV_WIN + tq, KV_COLS), F32), pltpu.VMEM((seqs * tq, D_MODEL), F32)],
    )
    return pl.pallas_call(
        kern,
        out_shape=jax.ShapeDtypeStruct((bsz, lp, D_MODEL), F32),
        grid_spec=grid_spec,
        compiler_params=pltpu.CompilerParams(dimension_semantics=("arbitrary", "arbitrary"),
                                             vmem_limit_bytes=VMEM_LIMIT),
        name="swa_ffn1",
    )(sinks, x3d, kv3d, kv_prev, nwm, wq, wo, nw, wg, wu, wd, nwf)


def _prep_params(p):
    q = {}
    w_in = p["w_in"][0]
    off_qkv = S5_WIDTH
    off_z = off_qkv + GDN_CONV_CH
    off_b = off_z + GDN_V
    w_perm = jnp.concatenate(
        [w_in[:, off_qkv:off_z], w_in[:, :S5_WIDTH], w_in[:, off_z:off_b], w_in[:, off_b:],
         jnp.zeros((D_MODEL, LANES - 2 * GDN_HEADS), F32)], axis=1)
    q["w_in"] = w_perm.astype(BF16)

    lam_re, lam_im = p["s5_lam_re"][0], p["s5_lam_im"][0]
    dt = jnp.exp(p["s5_log_dt"][0])[:, None]
    mag = jnp.exp(lam_re * dt)
    lbr, lbi = mag * jnp.cos(lam_im * dt), mag * jnp.sin(lam_im * dt)
    den = lam_re * lam_re + lam_im * lam_im
    nr, ni = lbr - 1.0, lbi
    cr = (nr * lam_re + ni * lam_im) / den
    ci = (ni * lam_re - nr * lam_im) / den
    b_re, b_im = p["s5_b_re"][0], p["s5_b_im"][0]
    bbr = cr[..., None] * b_re - ci[..., None] * b_im
    bbi = cr[..., None] * b_im + ci[..., None] * b_re
    gb = S5_GROUPS // S5_BLOCKS
    eye = jnp.eye(gb, dtype=F32)

    def in_blocks(m):
        m = m.reshape(S5_BLOCKS, gb, S5_P, S5_GROUP)
        return jnp.einsum("jgpc,gh->jgchp", m, eye).reshape(S5_BLOCKS, gb * S5_GROUP, gb * S5_P)

    def out_blocks(m):
        m = m.reshape(S5_BLOCKS, gb, S5_GROUP, S5_P)
        return jnp.einsum("jgcp,gh->jgphc", m, eye).reshape(S5_BLOCKS, gb * S5_P, gb * S5_GROUP)

    q["s5_wb"] = jnp.concatenate([in_blocks(bbr), in_blocks(bbi)], axis=2).astype(BF16)
    q["s5_wc"] = jnp.concatenate([out_blocks(p["s5_c_re"][0]), -out_blocks(p["s5_c_im"][0])], axis=1).astype(BF16)
    q["s5_lamr"] = lbr.reshape(1, S5_STATE)
    q["s5_lami"] = lbi.reshape(1, S5_STATE)
    q["s5_d"] = p["s5_d"][0].reshape(1, S5_WIDTH)
    q["s5_wglu"] = p["s5_w_glu"][0].astype(BF16)
    q["s5_bglu"] = p["s5_b_glu"][0].reshape(1, S5_WIDTH)

    def head_row(v):
        return jnp.zeros((1, LANES), F32).at[0, GDN_HEADS:2 * GDN_HEADS].set(v)

    q["gdn_alog"] = head_row(p["gdn_a_log"][0])
    q["gdn_dtb"] = head_row(p["gdn_dt_bias"][0])
    q["gdn_conv_w"] = p["gdn_conv_w"][0]
    q["gdn_nw"] = p["gdn_norm_w"][0].reshape(1, GDN_DV)
    q["w_out"] = p["w_out_ab"][0].astype(BF16)
    q["wq"] = p["swa_wq"][0].astype(BF16)
    q["wkv"] = jnp.concatenate([p["swa_wk"][0], p["swa_wv"][0]], axis=1).astype(BF16)
    q["wo"] = p["swa_wo"][0].astype(BF16)
    q["sinks"] = p["swa_sinks"][0]
    for name in ("norm_mix", "norm_ffn"):
        q[name] = p[name].reshape(-1, 1, D_MODEL)
    q["norm_final"] = p["norm_final"].reshape(1, D_MODEL)
    for name in ("ffn_w_gate", "ffn_w_up", "ffn_w_down"):
        q[name] = p[name].astype(BF16)
    return q


def _tile(n, target):
    t = min(n, target)
    while n % t:
        t //= 2
    return t


def _trunk(x, s5_re, s5_im, gdn_s, gdn_conv, cache_k, cache_v, q, n_valid, tiles):
    bsz, lp, _ = x.shape
    has_cache = cache_k is not None
    t = bsz * lp
    tm = _tile(t, tiles["tm"])
    x2d = x.reshape(t, D_MODEL)

    proj, qkv_tail = _in_proj(x, n_valid, q["norm_mix"][0], q["w_in"], q["gdn_conv_w"], gdn_conv, min(tm, lp))
    proj = proj.reshape(bsz, lp, PROJ_COLS)
    a_out, x_re, x_im = _s5(proj, n_valid, s5_re.reshape(bsz, S5_STATE), s5_im.reshape(bsz, S5_STATE),
                            q["s5_lamr"], q["s5_lami"], q["s5_wb"], q["s5_wc"], q["s5_d"], q["s5_wglu"],
                            q["s5_bglu"], _tile(n_valid, tiles["tc"]))
    if n_valid != lp:
        a_out = jnp.pad(a_out, ((0, 0), (0, lp - n_valid), (0, 0)))
    b_out, s_fin = _gdn(proj, n_valid, gdn_s, q["gdn_alog"], q["gdn_dtb"], q["gdn_nw"], _tile(lp, tiles["tt"]))
    new_buf = qkv_tail[:, -(GDN_CONV - 1):]

    x2, kv = _mix0(x2d, a_out.reshape(t, S5_WIDTH), b_out.reshape(t, GDN_V), q["w_out"], q["norm_ffn"][0],
                   q["ffn_w_gate"][0], q["ffn_w_up"][0], q["ffn_w_down"][0], q["norm_mix"][1], q["wkv"], tm)
    kv = kv.reshape(bsz, lp, KV_COLS)
    half = SWA_KV_HEADS * SWA_HD
    if has_cache:
        kv_prev = jnp.concatenate([cache_k.reshape(bsz, KV_WIN, half), cache_v.reshape(bsz, KV_WIN, half)], axis=-1)
        tq, cq = lp, n_valid
    else:
        assert n_valid == lp
        kv_prev = kv
        tq, cq = _tile(lp, tiles["tq"]), CHUNK
    y = _swa(q["sinks"], x2.reshape(bsz, lp, D_MODEL), kv, kv_prev, n_valid, has_cache, q["norm_mix"][1],
             q["wq"], q["wo"], q["norm_ffn"][1], q["ffn_w_gate"][1], q["ffn_w_up"][1], q["ffn_w_down"][1],
             q["norm_final"], tq, cq)

    kv_tail = kv[:, max(n_valid - KV_WIN, 0):n_valid]
    if has_cache:
        kv_tail = jnp.concatenate([kv_prev, kv_tail], axis=1)[:, -KV_WIN:]
    ck = kv_tail[:, :, :half].reshape(bsz, KV_WIN, SWA_KV_HEADS, SWA_HD)
    cv = kv_tail[:, :, half:].reshape(bsz, KV_WIN, SWA_KV_HEADS, SWA_HD)
    return (y if n_valid == lp else y[:, :n_valid], x_re.reshape(1, bsz, S5_GROUPS, S5_P), x_im.reshape(1, bsz, S5_GROUPS, S5_P),
            s_fin[None], new_buf[None], ck[None], cv[None])


PROMPT_TILES = dict(tm=512, tc=128, tt=256, tq=512)
SAMPLE_PAD = 64


def kernel(x_prompt, x_sample, state_s5_re, state_s5_im, state_gdn, state_gdn_conv, cache_swa_k, cache_swa_v,
           norm_mix, norm_ffn, norm_final, w_in, s5_lam_re, s5_lam_im, s5_log_dt, s5_b_re, s5_b_im,
           s5_c_re, s5_c_im, s5_d, s5_w_glu, s5_b_glu, gdn_conv_w, gdn_a_log, gdn_dt_bias, gdn_norm_w,
           w_out_ab, swa_wq, swa_wk, swa_wv, swa_sinks, swa_wo, ffn_w_gate, ffn_w_up, ffn_w_down):
    p = dict(norm_mix=norm_mix, norm_ffn=norm_ffn, norm_final=norm_final, w_in=w_in,
             s5_lam_re=s5_lam_re, s5_lam_im=s5_lam_im, s5_log_dt=s5_log_dt, s5_b_re=s5_b_re, s5_b_im=s5_b_im,
             s5_c_re=s5_c_re, s5_c_im=s5_c_im, s5_d=s5_d, s5_w_glu=s5_w_glu, s5_b_glu=s5_b_glu,
             gdn_conv_w=gdn_conv_w, gdn_a_log=gdn_a_log, gdn_dt_bias=gdn_dt_bias, gdn_norm_w=gdn_norm_w,
             w_out_ab=w_out_ab, swa_wq=swa_wq, swa_wk=swa_wk, swa_wv=swa_wv, swa_sinks=swa_sinks,
             swa_wo=swa_wo, ffn_w_gate=ffn_w_gate, ffn_w_up=ffn_w_up, ffn_w_down=ffn_w_down)
    q = _prep_params(p)
    bsz, seq, _ = x_prompt.shape
    dbsz, dseq, _ = x_sample.shape
    zeros_s5 = jnp.zeros((bsz, S5_GROUPS, S5_P), F32)
    outs_p = _trunk(x_prompt, zeros_s5, zeros_s5, jnp.zeros((bsz, GDN_HEADS, GDN_DK, GDN_DV), F32),
                    jnp.zeros((bsz, GDN_CONV - 1, GDN_CONV_CH), F32), None, None, q, seq, PROMPT_TILES)
    x_s = jnp.pad(x_sample, ((0, 0), (0, SAMPLE_PAD - dseq), (0, 0)))
    outs_s = _trunk(x_s, state_s5_re[0], state_s5_im[0], state_gdn[0], state_gdn_conv[0],
                    cache_swa_k[0], cache_swa_v[0], q, dseq, PROMPT_TILES)
    return (outs_p[0], outs_s[0]) + tuple(outs_p[1:]) + tuple(outs_s[1:])
```

```python
import functools
import math

import jax
import jax.numpy as jnp
from jax import lax
from jax.experimental import pallas as pl
from jax.experimental.pallas import tpu as pltpu

F32 = jnp.float32
BF16 = jnp.bfloat16

D_MODEL = 1024
CHUNK = 64
RMS_EPS = 1e-6
L2_EPS = 1e-6

S5_WIDTH = 512
S5_GROUP = 16
S5_GROUPS = 32
S5_P = 64
S5_STATE = S5_GROUPS * S5_P
S5_BLOCKS = 4
S5_BLOCK_STATE = S5_STATE // S5_BLOCKS

GDN_HEADS = 4
GDN_DK = 128
GDN_DV = 128
GDN_QK = GDN_HEADS * GDN_DK
GDN_V = GDN_HEADS * GDN_DV
GDN_CONV = 4
GDN_CONV_CH = 2 * GDN_QK + GDN_V
SOLVE_SPLIT_WIDTH = 8
GDN_ROWS_PER_STEP = 8
TAIL_ROWS = 8

SWA_HEADS = 16
SWA_KV_HEADS = 4
SWA_GROUPS = 4
SWA_HD = 64
KV_WIN = 128
KV_COLS = 2 * SWA_KV_HEADS * SWA_HD
SWA_ROWS_PER_STEP = 256
SWA_STAGE_CHUNKS = 2

D_FF = 2816
FF_CHUNK = 256

PROJ_QKV = 0
PROJ_U = GDN_CONV_CH
PROJ_Z = PROJ_U + S5_WIDTH
PROJ_BA = PROJ_Z + GDN_V
PROJ_COLS = PROJ_BA + 128

LANES = 128
VMEM_LIMIT = 56 * 1024 * 1024


def _rms(x, w):
    return x * lax.rsqrt(jnp.mean(x * x, axis=-1, keepdims=True) + RMS_EPS) * w


def _dot(a, b):
    return jnp.dot(a.astype(BF16), b.astype(BF16), preferred_element_type=F32)


def _dot_nt(a, b):
    return lax.dot_general(a.astype(BF16), b.astype(BF16), (((1,), (1,)), ((), ())),
                           preferred_element_type=F32)


def _const_spec(shape):
    return pl.BlockSpec(shape, lambda *_: (0,) * len(shape), pipeline_mode=pl.Buffered(1))


def _in_proj_kernel(x_ref, nw_ref, w_ref, cw_ref, buf_ref, o_ref, tail_ref, xp, *, tm, tiles_per_seq, tail_tile,
                    tail_off):
    it = pl.program_id(0) % tiles_per_seq
    pad = TAIL_ROWS

    @pl.when(it == 0)
    def _():
        xp[0:pad, :] = jnp.zeros((pad, GDN_CONV_CH), F32)
        xp[pad - (GDN_CONV - 1):pad, :] = buf_ref[...]

    p = _dot(_rms(x_ref[...], nw_ref[...]), w_ref[...])
    o_ref[:, GDN_CONV_CH:] = p[:, GDN_CONV_CH:]
    xp[pad:pad + tm, :] = p[:, :GDN_CONV_CH]

    y = xp[pad:pad + tm, :] * cw_ref[GDN_CONV - 1:GDN_CONV, :]
    for j in range(GDN_CONV - 1):
        off = pad - (GDN_CONV - 1) + j
        y = y + xp[off:off + tm, :] * cw_ref[j:j + 1, :]
    xp[0:pad, :] = xp[tm:tm + pad, :]
    y = jax.nn.silu(y)
    for h in range(2 * GDN_HEADS):
        cols = slice(h * GDN_DK, (h + 1) * GDN_DK)
        x_h = y[:, cols]
        scale = GDN_DK ** -0.5 if h < GDN_HEADS else 1.0
        o_ref[:, cols] = x_h * (lax.rsqrt(jnp.sum(x_h * x_h, axis=-1, keepdims=True) + L2_EPS) * scale)
    o_ref[:, 2 * GDN_QK:GDN_CONV_CH] = y[:, 2 * GDN_QK:]

    @pl.when(it == tail_tile)
    def _():
        tail_ref[...] = xp[pad + tail_off:pad + tail_off + TAIL_ROWS, :]


def _in_proj(x3d, n_valid, norm_w, w_perm, conv_w, conv_buf, tm):
    bsz, lp, _ = x3d.shape
    t = bsz * lp
    tiles_per_seq = lp // tm
    assert n_valid >= TAIL_ROWS and (n_valid - TAIL_ROWS) // tm == (n_valid - 1) // tm
    tail_tile = (n_valid - 1) // tm
    kern = functools.partial(_in_proj_kernel, tm=tm, tiles_per_seq=tiles_per_seq, tail_tile=tail_tile,
                             tail_off=n_valid - TAIL_ROWS - tail_tile * tm)
    return pl.pallas_call(
        kern,
        out_shape=(jax.ShapeDtypeStruct((t, PROJ_COLS), F32),
                   jax.ShapeDtypeStruct((bsz, TAIL_ROWS, GDN_CONV_CH), F32)),
        grid=(t // tm,),
        in_specs=[pl.BlockSpec((tm, D_MODEL), lambda i: (i, 0)),
                  _const_spec((1, D_MODEL)),
                  _const_spec((D_MODEL, PROJ_COLS)),
                  _const_spec((GDN_CONV, GDN_CONV_CH)),
                  pl.BlockSpec((None, GDN_CONV - 1, GDN_CONV_CH), lambda i: (i // tiles_per_seq, 0, 0))],
        out_specs=(pl.BlockSpec((tm, PROJ_COLS), lambda i: (i, 0)),
                   pl.BlockSpec((None, TAIL_ROWS, GDN_CONV_CH), lambda i: (i // tiles_per_seq, 0, 0))),
        scratch_shapes=[pltpu.VMEM((tm + TAIL_ROWS, GDN_CONV_CH), F32)],
        compiler_params=pltpu.CompilerParams(dimension_semantics=("arbitrary",),
                                             vmem_limit_bytes=VMEM_LIMIT),
        name="in_proj",
    )(x3d.reshape(t, D_MODEL), norm_w, w_perm, conv_w, conv_buf)


def _s5_kernel(u_ref, x0r_ref, x0i_ref, lamr_ref, lami_ref, wb_ref, wc_ref, d_ref, wglu_ref, bglu_ref,
               o_ref, xr_out, xi_out, st_r, st_i, xs, *, tc, bsz):
    i = pl.program_id(0)

    @pl.when(i == 0)
    def _():
        st_r[...] = x0r_ref[...]
        st_i[...] = x0i_ref[...]

    u_tm = jnp.swapaxes(u_ref[...], 0, 1).reshape(tc * bsz, S5_WIDTH)
    half = S5_BLOCK_STATE
    ys = []
    for j in range(S5_BLOCKS):
        bu = _dot(u_tm[:, j * LANES:(j + 1) * LANES], wb_ref[j])
        xs[j] = bu.reshape(tc, bsz, 2 * half)
        lr = jnp.broadcast_to(lamr_ref[:, j * half:(j + 1) * half], (bsz, half))
        li = jnp.broadcast_to(lami_ref[:, j * half:(j + 1) * half], (bsz, half))
        xr = st_r[:, j * half:(j + 1) * half]
        xi = st_i[:, j * half:(j + 1) * half]
        for t in range(tc):
            xr, xi = (lr * xr - li * xi + xs[j, t, :, :half], lr * xi + li * xr + xs[j, t, :, half:])
            xs[j, t, :, :half] = xr
            xs[j, t, :, half:] = xi
        st_r[:, j * half:(j + 1) * half] = xr
        st_i[:, j * half:(j + 1) * half] = xi
        ys.append(_dot(xs[j].reshape(tc * bsz, 2 * half), wc_ref[j]))
    y = jnp.concatenate(ys, axis=-1) + d_ref[...] * u_tm
    z = jax.nn.gelu(y)
    out = z * jax.nn.sigmoid(_dot(z, wglu_ref[...]) + bglu_ref[...])
    o_ref[...] = jnp.swapaxes(out.reshape(tc, bsz, S5_WIDTH), 0, 1)
    xr_out[...] = st_r[...]
    xi_out[...] = st_i[...]


def _s5(proj3, n_valid, x0r, x0i, lamr, lami, wb, wc, d_row, wglu, bglu_row, tc):
    bsz = proj3.shape[0]
    kern = functools.partial(_s5_kernel, tc=tc, bsz=bsz)
    return pl.pallas_call(
        kern,
        out_shape=(jax.ShapeDtypeStruct((bsz, n_valid, S5_WIDTH), F32),
                   jax.ShapeDtypeStruct((bsz, S5_STATE), F32),
                   jax.ShapeDtypeStruct((bsz, S5_STATE), F32)),
        grid=(n_valid // tc,),
        in_specs=[pl.BlockSpec((bsz, tc, S5_WIDTH), lambda i: (0, i, PROJ_U // S5_WIDTH)),
                  _const_spec((bsz, S5_STATE)), _const_spec((bsz, S5_STATE)),
                  _const_spec((1, S5_STATE)), _const_spec((1, S5_STATE)),
                  _const_spec((S5_BLOCKS, LANES, 2 * S5_BLOCK_STATE)),
                  _const_spec((S5_BLOCKS, 2 * S5_BLOCK_STATE, LANES)),
                  _const_spec((1, S5_WIDTH)), _const_spec((S5_WIDTH, S5_WIDTH)), _const_spec((1, S5_WIDTH))],
        out_specs=(pl.BlockSpec((bsz, tc, S5_WIDTH), lambda i: (0, i, 0)),
                   pl.BlockSpec((bsz, S5_STATE), lambda i: (0, 0)),
                   pl.BlockSpec((bsz, S5_STATE), lambda i: (0, 0))),
        scratch_shapes=[pltpu.VMEM((bsz, S5_STATE), F32), pltpu.VMEM((bsz, S5_STATE), F32),
                        pltpu.VMEM((S5_BLOCKS, tc, bsz, 2 * S5_BLOCK_STATE), F32)],
        compiler_params=pltpu.CompilerParams(dimension_semantics=("arbitrary",),
                                             vmem_limit_bytes=VMEM_LIMIT),
        name="s5_mixer",
    )(proj3, x0r, x0i, lamr, lami, wb, wc, d_row, wglu, bglu_row)


def _solve_unit_lower(a_strict, rhs, block):
    n = len(rhs)
    w = rhs[0].shape[1]

    def hi_lo(x):
        hi = x.astype(BF16)
        return [hi, (x - hi.astype(F32)).astype(BF16)]

    m = [(-a).astype(BF16) for a in a_strict]
    x = list(rhs)
    width = 1
    while width < block:
        last = 2 * width >= block
        if width < SOLVE_SPLIT_WIDTH:
            p = [jnp.dot(m[j], jnp.concatenate(hi_lo(x[j]) + ([] if last else [m[j]]), axis=1),
                         preferred_element_type=F32) for j in range(n)]
            x = [x[j] + (p[j][:, :w] + p[j][:, w:2 * w]) for j in range(n)]
            if not last:
                m = [p[j][:, 2 * w:].astype(BF16) for j in range(n)]
        else:
            x = [x[j] + jnp.dot(m[j], x[j].astype(BF16), preferred_element_type=F32) for j in range(n)]
            if not last:
                m = [jnp.dot(m[j], m[j], preferred_element_type=F32).astype(BF16) for j in range(n)]
        width *= 2
    return x


def _gdn_kernel(qkv_ref, z_ref, ba_ref, s0_ref, alog_ref, dtb_ref, nw_ref, o_ref, s_out, s_st, *, tt, n_valid,
                rows_per_step):
    i = pl.program_id(1)
    nc = tt // CHUNK

    @pl.when(i == 0)
    def _():
        s_st[...] = s0_ref[...]

    ri = lax.broadcasted_iota(jnp.int32, (tt, tt), 0)
    ci = lax.broadcasted_iota(jnp.int32, (tt, tt), 1)
    same = (ri // CHUNK) == (ci // CHUNK)
    incl = jnp.logical_and(same, ri >= ci)
    strict = jnp.logical_and(same, ri > ci)
    row = lax.broadcasted_iota(jnp.int32, (tt, LANES), 0)
    pos = row % CHUNK

    units = [(r, h) for r in range(rows_per_step) for h in range(GDN_HEADS)]
    gates = []
    for r in range(rows_per_step):
        ba = ba_ref[r]
        beta_all = jax.nn.sigmoid(ba)
        g_all = -jnp.exp(alog_ref[...]) * jax.nn.softplus(ba + dtb_ref[...])
        if n_valid % tt != 0:
            live = (i * tt + row) < n_valid
            beta_all = jnp.where(live, beta_all, 0.0)
            g_all = jnp.where(live, g_all, 0.0)
        gc_all = g_all
        shift = 1
        while shift < CHUNK:
            gc_all = gc_all + jnp.where(pos >= shift, pltpu.roll(gc_all, shift, axis=0), 0.0)
            shift *= 2
        g_end_all = jnp.concatenate(
            [jnp.broadcast_to(gc_all[c * CHUNK + CHUNK - 1:(c + 1) * CHUNK, :], (CHUNK, LANES)) for c in range(nc)],
            axis=0)
        gates.append((beta_all, gc_all, g_end_all, gc_all.T, (g_end_all - gc_all).T))

    kq, kb, kt, vb, qn = [], [], [], [], []
    for r, h in units:
        q_h = qkv_ref[r, :, h * GDN_DK:(h + 1) * GDN_DK]
        k_h = qkv_ref[r, :, GDN_QK + h * GDN_DK:GDN_QK + (h + 1) * GDN_DK]
        v_h = qkv_ref[r, :, 2 * GDN_QK + h * GDN_DV:2 * GDN_QK + (h + 1) * GDN_DV]
        beta_b = jnp.broadcast_to(gates[r][0][:, h:h + 1], (tt, GDN_DK))
        kb.append(k_h * beta_b)
        vb.append(v_h * beta_b)
        kt.append(k_h.T)
        qn.append(q_h)
        kq.append(_dot(jnp.concatenate([kb[-1], q_h], axis=0), kt[-1]))
    a_mat, attn, rhs, q_dec, kdt = [], [], [], [], []
    for n, (r, h) in enumerate(units):
        _, gc_all, _, gc_t, rest_t = gates[r]
        gc_b = jnp.broadcast_to(gc_all[:, GDN_HEADS + h:GDN_HEADS + h + 1], (tt, max(tt, GDN_DK)))
        gexp_b = jnp.exp(gc_b[:, :GDN_DK])
        grow = gc_t[GDN_HEADS + h:GDN_HEADS + h + 1, :]
        decay = jnp.where(incl, jnp.exp(jnp.where(incl, gc_b[:, :tt] - grow, 0.0)), 0.0)
        a_mat.append(jnp.where(strict, kq[n][:tt] * decay, 0.0))
        attn.append(kq[n][tt:] * decay)
        rhs.append(jnp.concatenate([vb[n], kb[n] * gexp_b], axis=1))
        q_dec.append(qn[n] * gexp_b)
        kdt.append(kt[n] * jnp.exp(rest_t[GDN_HEADS + h:GDN_HEADS + h + 1, :]))
    uw = _solve_unit_lower(a_mat, rhs, CHUNK)
    u_val = [x[:, :GDN_DV] for x in uw]
    w_key = [x[:, GDN_DV:] for x in uw]

    s = [s_st[r, h] for r, h in units]
    o_state = [[] for _ in units]
    v_new = [[] for _ in units]
    for c in range(nc):
        rows = slice(c * CHUNK, (c + 1) * CHUNK)
        wq = [_dot(jnp.concatenate([w_key[n][rows], q_dec[n][rows]], axis=0), s[n]) for n in range(len(units))]
        for n in range(len(units)):
            o_state[n].append(wq[n][CHUNK:])
            v_new[n].append(u_val[n][rows] - wq[n][:CHUNK])
        upd = [_dot(kdt[n][:, rows], v_new[n][c]) for n in range(len(units))]
        for n, (r, h) in enumerate(units):
            g_last = jnp.exp(gates[r][2][c * CHUNK:c * CHUNK + 1, GDN_HEADS + h:GDN_HEADS + h + 1])
            s[n] = s[n] * g_last + upd[n]
    o_all = [jnp.concatenate(o_state[n], axis=0) + _dot(attn[n], jnp.concatenate(v_new[n], axis=0))
             for n in range(len(units))]
    for n, (r, h) in enumerate(units):
        s_st[r, h] = s[n]
        s_out[r, h] = s[n]
        o = _rms(o_all[n], nw_ref[...])
        o_ref[r, :, h * GDN_DV:(h + 1) * GDN_DV] = o * jax.nn.silu(z_ref[r, :, h * GDN_DV:(h + 1) * GDN_DV])


def _gdn(proj3, n_valid, s0, alog_row, dtb_row, nw_row, tt):
    bsz, lp, _ = proj3.shape
    rps = GDN_ROWS_PER_STEP
    kern = functools.partial(_gdn_kernel, tt=tt, n_valid=n_valid, rows_per_step=rps)
    return pl.pallas_call(
        kern,
        out_shape=(jax.ShapeDtypeStruct((bsz, lp, GDN_V), F32),
                   jax.ShapeDtypeStruct((bsz, GDN_HEADS, GDN_DK, GDN_DV), F32)),
        grid=(bsz // rps, lp // tt),
        in_specs=[pl.BlockSpec((rps, tt, GDN_CONV_CH), lambda b, i: (b, i, PROJ_QKV // GDN_CONV_CH)),
                  pl.BlockSpec((rps, tt, GDN_V), lambda b, i: (b, i, PROJ_Z // GDN_V)),
                  pl.BlockSpec((rps, tt, LANES), lambda b, i: (b, i, PROJ_BA // LANES)),
                  pl.BlockSpec((rps, GDN_HEADS, GDN_DK, GDN_DV), lambda b, i: (b, 0, 0, 0)),
                  _const_spec((1, LANES)), _const_spec((1, LANES)), _const_spec((1, GDN_DV))],
        out_specs=(pl.BlockSpec((rps, tt, GDN_V), lambda b, i: (b, i, 0)),
                   pl.BlockSpec((rps, GDN_HEADS, GDN_DK, GDN_DV), lambda b, i: (b, 0, 0, 0))),
        scratch_shapes=[pltpu.VMEM((rps, GDN_HEADS, GDN_DK, GDN_DV), F32)],
        compiler_params=pltpu.CompilerParams(dimension_semantics=("arbitrary", "arbitrary"),
                                             vmem_limit_bytes=VMEM_LIMIT),
        name="gdn_mixer",
    )(proj3, proj3, proj3, s0, alog_row, dtb_row, nw_row)


def _swiglu(x, nw_ref, wg_ref, wu_ref, wd_ref):
    h = _rms(x, nw_ref[...]).astype(BF16)
    acts = []
    for f in range(0, D_FF, FF_CHUNK):
        g = jnp.dot(h, wg_ref[:, f:f + FF_CHUNK], preferred_element_type=F32)
        u = jnp.dot(h, wu_ref[:, f:f + FF_CHUNK], preferred_element_type=F32)
        acts.append((jax.nn.silu(g) * u).astype(BF16))
    return x + jnp.dot(jnp.concatenate(acts, axis=1), wd_ref[...], preferred_element_type=F32)


def _ffn_specs():
    return [_const_spec((1, D_MODEL)), _const_spec((D_MODEL, D_FF)), _const_spec((D_MODEL, D_FF)),
            _const_spec((D_FF, D_MODEL))]


def _mix0_kernel(x_ref, a_ref, b_ref, wo_ref, nw_ref, wg_ref, wu_ref, wd_ref, nw1_ref, wkv_ref,
                 o_ref, kv_ref):
    mix = _dot(a_ref[...], wo_ref[:S5_WIDTH, :]) + _dot(b_ref[...], wo_ref[S5_WIDTH:, :])
    x2 = _swiglu(x_ref[...] + mix, nw_ref, wg_ref, wu_ref, wd_ref)
    o_ref[...] = x2
    kv_ref[...] = _dot(_rms(x2, nw1_ref[...]), wkv_ref[...])


def _mix0(x2d, a2d, b2d, wo, nw, wg, wu, wd, nw1, wkv, tm):
    t = x2d.shape[0]
    return pl.pallas_call(
        _mix0_kernel,
        out_shape=(jax.ShapeDtypeStruct((t, D_MODEL), F32), jax.ShapeDtypeStruct((t, KV_COLS), F32)),
        grid=(t // tm,),
        in_specs=[pl.BlockSpec((tm, D_MODEL), lambda i: (i, 0)),
                  pl.BlockSpec((tm, S5_WIDTH), lambda i: (i, 0)),
                  pl.BlockSpec((tm, GDN_V), lambda i: (i, 0)),
                  _const_spec((S5_WIDTH + GDN_V, D_MODEL))] + _ffn_specs() +
                 [_const_spec((1, D_MODEL)), _const_spec((D_MODEL, KV_COLS))],
        out_specs=(pl.BlockSpec((tm, D_MODEL), lambda i: (i, 0)),
                   pl.BlockSpec((tm, KV_COLS), lambda i: (i, 0))),
        compiler_params=pltpu.CompilerParams(dimension_semantics=("arbitrary",),
                                             vmem_limit_bytes=VMEM_LIMIT),
        name="mix0_ffn0",
    )(x2d, a2d, b2d, wo, nw, wg, wu, wd, nw1, wkv)


def _swa_kernel(sink_ref, x_ref, kv_ref, kvp_ref, nwm_ref, wq_ref, wo_ref, nw_ref, wg_ref, wu_ref, wd_ref,
                nwf_ref, o_ref, kvc, att, *, tq, cq, n_valid, has_cache, seqs):
    i = pl.program_id(1)
    x = x_ref[...].reshape(seqs * tq, D_MODEL)
    q = (_dot(_rms(x, nwm_ref[...]), wq_ref[...]) * (SWA_HD ** -0.5)).astype(BF16)
    win = KV_WIN + cq
    rows_all = KV_WIN + tq
    low = lax.broadcasted_iota(jnp.int32, (rows_all, LANES), 1) < SWA_HD
    col = lax.broadcasted_iota(jnp.int32, (2 * cq, win), 1)
    top = lax.broadcasted_iota(jnp.int32, (2 * cq, 1), 0) < cq
    n_chunks = 1 if has_cache else tq // cq
    if n_chunks * cq < tq:
        att[...] = jnp.zeros((seqs * tq, D_MODEL), F32)

    def soft(sc, sink):
        m = jnp.maximum(jnp.max(sc, axis=-1, keepdims=True), sink)
        p = jnp.exp(sc - m)
        return p.astype(BF16), 1.0 / (jnp.sum(p, axis=-1, keepdims=True) + jnp.exp(sink - m))

    for sq in range(seqs):
        kvc[sq, 0:KV_WIN, :] = kvp_ref[sq]
        kvc[sq, KV_WIN:KV_WIN + tq, :] = kv_ref[sq]
        k_lo, k_hi, v_lo, v_hi = [], [], [], []
        for dst_lo, dst_hi, base in ((k_lo, k_hi, 0), (v_lo, v_hi, SWA_KV_HEADS * SWA_HD)):
            for j in range(SWA_KV_HEADS // 2):
                t = kvc[sq, :, base + j * LANES:base + (j + 1) * LANES]
                r = pltpu.roll(t, SWA_HD, axis=1)
                dst_lo += [jnp.where(low, t, 0.0).astype(BF16), jnp.where(low, r, 0.0).astype(BF16)]
                dst_hi += [jnp.where(low, 0.0, r).astype(BF16), jnp.where(low, 0.0, t).astype(BF16)]
        for c0 in range(0, n_chunks, SWA_STAGE_CHUNKS):
            units = [(c, kh) for c in range(c0, min(c0 + SWA_STAGE_CHUNKS, n_chunks)) for kh in range(SWA_KV_HEADS)]
            scores = []
            for c, kh in units:
                r0 = sq * tq + c * cq
                qs = jnp.concatenate([q[r0:r0 + cq, (2 * kh) * LANES:(2 * kh + 1) * LANES],
                                      q[r0:r0 + cq, (2 * kh + 1) * LANES:(2 * kh + 2) * LANES]], axis=0)
                keys = slice(c * cq, c * cq + win)
                scores.append((_dot_nt(qs, k_lo[kh][keys]), _dot_nt(qs, k_hi[kh][keys])))
            probs = []
            for (c, kh), (s_e, s_o) in zip(units, scores):
                key_row = col + c * cq
                if has_cache:
                    ok = key_row < KV_WIN + n_valid
                elif c * cq < KV_WIN:
                    ok = jnp.logical_or(key_row >= KV_WIN, i > 0)
                else:
                    ok = None
                if ok is not None:
                    s_e, s_o = jnp.where(ok, s_e, -1e30), jnp.where(ok, s_o, -1e30)
                h0 = kh * SWA_GROUPS
                probs.append(soft(s_e, jnp.where(top, sink_ref[h0], sink_ref[h0 + 2]))
                             + soft(s_o, jnp.where(top, sink_ref[h0 + 1], sink_ref[h0 + 3])))
            for (c, kh), (p_e, inv_e, p_o, inv_o) in zip(units, probs):
                r0 = sq * tq + c * cq
                keys = slice(c * cq, c * cq + win)
                o = inv_e * jnp.dot(p_e, v_lo[kh][keys], preferred_element_type=F32) \
                    + inv_o * jnp.dot(p_o, v_hi[kh][keys], preferred_element_type=F32)
                att[r0:r0 + cq, (2 * kh) * LANES:(2 * kh + 1) * LANES] = o[:cq]
                att[r0:r0 + cq, (2 * kh + 1) * LANES:(2 * kh + 2) * LANES] = o[cq:]
    x3 = x + _dot(att[...], wo_ref[...])
    x4 = _swiglu(x3, nw_ref, wg_ref, wu_ref, wd_ref)
    o_ref[...] = _rms(x4, nwf_ref[...]).reshape(seqs, tq, D_MODEL)


def _swa(sinks, x3d, kv3d, kv_prev, n_valid, has_cache, nwm, wq, wo, nw, wg, wu, wd, nwf, tq, cq):
    bsz, lp, _ = x3d.shape
    seqs = max(1, min(bsz, SWA_ROWS_PER_STEP // tq)) if has_cache else 1
    assert bsz % seqs == 0
    kern = functools.partial(_swa_kernel, tq=tq, cq=cq, n_valid=n_valid, has_cache=has_cache, seqs=seqs)
    per = tq // KV_WIN
    if has_cache:
        prev_map = lambda b, i, s: (b, 0, 0)
    else:
        prev_map = lambda b, i, s: (b, jnp.maximum(i * per - 1, 0), 0)
    grid_spec = pltpu.PrefetchScalarGridSpec(
        num_scalar_prefetch=1,
        grid=(bsz // seqs, lp // tq),
        in_specs=[pl.BlockSpec((seqs, tq, D_MODEL), lambda b, i, s: (b, i, 0)),
                  pl.BlockSpec((seqs, tq, KV_COLS), lambda b, i, s: (b, i, 0)),
                  pl.BlockSpec((seqs, KV_WIN, KV_COLS), prev_map),
                  _const_spec((1, D_MODEL)), _const_spec((D_MODEL, D_MODEL)), _const_spec((D_MODEL, D_MODEL))]
                 + _ffn_specs() + [_const_spec((1, D_MODEL))],
        out_specs=pl.BlockSpec((seqs, tq, D_MODEL), lambda b, i, s: (b, i, 0)),
        scratch_shapes=[pltpu.VMEM((seqs, KV_WIN + tq, KV_COLS), F32), pltpu.VMEM((seqs * tq, D_MODEL), F32)],
    )
    return pl.pallas_call(
        kern,
        out_shape=jax.ShapeDtypeStruct((bsz, lp, D_MODEL), F32),
        grid_spec=grid_spec,
        compiler_params=pltpu.CompilerParams(dimension_semantics=("arbitrary", "arbitrary"),
                                             vmem_limit_bytes=VMEM_LIMIT),
        name="swa_ffn1",
    )(sinks, x3d, kv3d, kv_prev, nwm, wq, wo, nw, wg, wu, wd, nwf)


def _prep_params(p):
    q = {}
    w_in = p["w_in"][0]
    off_qkv = S5_WIDTH
    off_z = off_qkv + GDN_CONV_CH
    off_b = off_z + GDN_V
    w_perm = jnp.concatenate(
        [w_in[:, off_qkv:off_z], w_in[:, :S5_WIDTH], w_in[:, off_z:off_b], w_in[:, off_b:],
         jnp.zeros((D_MODEL, LANES - 2 * GDN_HEADS), F32)], axis=1)
    q["w_in"] = w_perm.astype(BF16)

    lam_re, lam_im = p["s5_lam_re"][0], p["s5_lam_im"][0]
    dt = jnp.exp(p["s5_log_dt"][0])[:, None]
    mag = jnp.exp(lam_re * dt)
    lbr, lbi = mag * jnp.cos(lam_im * dt), mag * jnp.sin(lam_im * dt)
    den = lam_re * lam_re + lam_im * lam_im
    nr, ni = lbr - 1.0, lbi
    cr = (nr * lam_re + ni * lam_im) / den
    ci = (ni * lam_re - nr * lam_im) / den
    b_re, b_im = p["s5_b_re"][0], p["s5_b_im"][0]
    bbr = cr[..., None] * b_re - ci[..., None] * b_im
    bbi = cr[..., None] * b_im + ci[..., None] * b_re
    gb = S5_GROUPS // S5_BLOCKS
    eye = jnp.eye(gb, dtype=F32)

    def in_blocks(m):
        m = m.reshape(S5_BLOCKS, gb, S5_P, S5_GROUP)
        return jnp.einsum("jgpc,gh->jgchp", m, eye).reshape(S5_BLOCKS, gb * S5_GROUP, gb * S5_P)

    def out_blocks(m):
        m = m.reshape(S5_BLOCKS, gb, S5_GROUP, S5_P)
        return jnp.einsum("jgcp,gh->jgphc", m, eye).reshape(S5_BLOCKS, gb * S5_P, gb * S5_GROUP)

    q["s5_wb"] = jnp.concatenate([in_blocks(bbr), in_blocks(bbi)], axis=2).astype(BF16)
    q["s5_wc"] = jnp.concatenate([out_blocks(p["s5_c_re"][0]), -out_blocks(p["s5_c_im"][0])], axis=1).astype(BF16)
    q["s5_lamr"] = lbr.reshape(1, S5_STATE)
    q["s5_lami"] = lbi.reshape(1, S5_STATE)
    q["s5_d"] = p["s5_d"][0].reshape(1, S5_WIDTH)
    q["s5_wglu"] = p["s5_w_glu"][0].astype(BF16)
    q["s5_bglu"] = p["s5_b_glu"][0].reshape(1, S5_WIDTH)

    def head_row(v):
        return jnp.zeros((1, LANES), F32).at[0, GDN_HEADS:2 * GDN_HEADS].set(v)

    q["gdn_alog"] = head_row(p["gdn_a_log"][0])
    q["gdn_dtb"] = head_row(p["gdn_dt_bias"][0])
    q["gdn_conv_w"] = p["gdn_conv_w"][0]
    q["gdn_nw"] = p["gdn_norm_w"][0].reshape(1, GDN_DV)
    q["w_out"] = p["w_out_ab"][0].astype(BF16)
    q["wq"] = p["swa_wq"][0].astype(BF16)
    q["wkv"] = jnp.concatenate([p["swa_wk"][0], p["swa_wv"][0]], axis=1).astype(BF16)
    q["wo"] = p["swa_wo"][0].astype(BF16)
    q["sinks"] = p["swa_sinks"][0]
    for name in ("norm_mix", "norm_ffn"):
        q[name] = p[name].reshape(-1, 1, D_MODEL)
    q["norm_final"] = p["norm_final"].reshape(1, D_MODEL)
    for name in ("ffn_w_gate", "ffn_w_up", "ffn_w_down"):
        q[name] = p[name].astype(BF16)
    return q


def _tile(n, target):
    t = min(n, target)
    while n % t:
        t //= 2
    return t


def _trunk(x, s5_re, s5_im, gdn_s, gdn_conv, cache_k, cache_v, q, n_valid, tiles):
    bsz, lp, _ = x.shape
    has_cache = cache_k is not None
    t = bsz * lp
    tm = _tile(t, tiles["tm"])
    x2d = x.reshape(t, D_MODEL)

    proj, qkv_tail = _in_proj(x, n_valid, q["norm_mix"][0], q["w_in"], q["gdn_conv_w"], gdn_conv, min(tm, lp))
    proj = proj.reshape(bsz, lp, PROJ_COLS)
    a_out, x_re, x_im = _s5(proj, n_valid, s5_re.reshape(bsz, S5_STATE), s5_im.reshape(bsz, S5_STATE),
                            q["s5_lamr"], q["s5_lami"], q["s5_wb"], q["s5_wc"], q["s5_d"], q["s5_wglu"],
                            q["s5_bglu"], _tile(n_valid, tiles["tc"]))
    if n_valid != lp:
        a_out = jnp.pad(a_out, ((0, 0), (0, lp - n_valid), (0, 0)))
    b_out, s_fin = _gdn(proj, n_valid, gdn_s, q["gdn_alog"], q["gdn_dtb"], q["gdn_nw"], _tile(lp, tiles["tt"]))
    new_buf = qkv_tail[:, -(GDN_CONV - 1):]

    x2, kv = _mix0(x2d, a_out.reshape(t, S5_WIDTH), b_out.reshape(t, GDN_V), q["w_out"], q["norm_ffn"][0],
                   q["ffn_w_gate"][0], q["ffn_w_up"][0], q["ffn_w_down"][0], q["norm_mix"][1], q["wkv"], tm)
    kv = kv.reshape(bsz, lp, KV_COLS)
    half = SWA_KV_HEADS * SWA_HD
    if has_cache:
        kv_prev = jnp.concatenate([cache_k.reshape(bsz, KV_WIN, half), cache_v.reshape(bsz, KV_WIN, half)], axis=-1)
        tq, cq = lp, n_valid
    else:
        assert n_valid == lp
        kv_prev = kv
        tq, cq = _tile(lp, tiles["tq"]), CHUNK
    y = _swa(q["sinks"], x2.reshape(bsz, lp, D_MODEL), kv, kv_prev, n_valid, has_cache, q["norm_mix"][1],
             q["wq"], q["wo"], q["norm_ffn"][1], q["ffn_w_gate"][1], q["ffn_w_up"][1], q["ffn_w_down"][1],
             q["norm_final"], tq, cq)

    kv_tail = kv[:, max(n_valid - KV_WIN, 0):n_valid]
    if has_cache:
        kv_tail = jnp.concatenate([kv_prev, kv_tail], axis=1)[:, -KV_WIN:]
    ck = kv_tail[:, :, :half].reshape(bsz, KV_WIN, SWA_KV_HEADS, SWA_HD)
    cv = kv_tail[:, :, half:].reshape(bsz, KV_WIN, SWA_KV_HEADS, SWA_HD)
    return (y if n_valid == lp else y[:, :n_valid], x_re.reshape(1, bsz, S5_GROUPS, S5_P), x_im.reshape(1, bsz, S5_GROUPS, S5_P),
            s_fin[None], new_buf[None], ck[None], cv[None])


PROMPT_TILES = dict(tm=512, tc=128, tt=128, tq=512)
SAMPLE_PAD = 64


def kernel(x_prompt, x_sample, state_s5_re, state_s5_im, state_gdn, state_gdn_conv, cache_swa_k, cache_swa_v,
           norm_mix, norm_ffn, norm_final, w_in, s5_lam_re, s5_lam_im, s5_log_dt, s5_b_re, s5_b_im,
           s5_c_re, s5_c_im, s5_d, s5_w_glu, s5_b_glu, gdn_conv_w, gdn_a_log, gdn_dt_bias, gdn_norm_w,
           w_out_ab, swa_wq, swa_wk, swa_wv, swa_sinks, swa_wo, ffn_w_gate, ffn_w_up, ffn_w_down):
    p = dict(norm_mix=norm_mix, norm_ffn=norm_ffn, norm_final=norm_final, w_in=w_in,
             s5_lam_re=s5_lam_re, s5_lam_im=s5_lam_im, s5_log_dt=s5_log_dt, s5_b_re=s5_b_re, s5_b_im=s5_b_im,
             s5_c_re=s5_c_re, s5_c_im=s5_c_im, s5_d=s5_d, s5_w_glu=s5_w_glu, s5_b_glu=s5_b_glu,
             gdn_conv_w=gdn_conv_w, gdn_a_log=gdn_a_log, gdn_dt_bias=gdn_dt_bias, gdn_norm_w=gdn_norm_w,
             w_out_ab=w_out_ab, swa_wq=swa_wq, swa_wk=swa_wk, swa_wv=swa_wv, swa_sinks=swa_sinks,
             swa_wo=swa_wo, ffn_w_gate=ffn_w_gate, ffn_w_up=ffn_w_up, ffn_w_down=ffn_w_down)
    q = _prep_params(p)
    bsz, seq, _ = x_prompt.shape
    dbsz, dseq, _ = x_sample.shape
    zeros_s5 = jnp.zeros((bsz, S5_GROUPS, S5_P), F32)
    outs_p = _trunk(x_prompt, zeros_s5, zeros_s5, jnp.zeros((bsz, GDN_HEADS, GDN_DK, GDN_DV), F32),
                    jnp.zeros((bsz, GDN_CONV - 1, GDN_CONV_CH), F32), None, None, q, seq, PROMPT_TILES)
    x_s = jnp.pad(x_sample, ((0, 0), (0, SAMPLE_PAD - dseq), (0, 0)))
    outs_s = _trunk(x_s, state_s5_re[0], state_s5_im[0], state_gdn[0], state_gdn_conv[0],
                    cache_swa_k[0], cache_swa_v[0], q, dseq, PROMPT_TILES)
    return (outs_p[0], outs_s[0]) + tuple(outs_p[1:]) + tuple(outs_s[1:])
```

```python
import functools
import math

import jax
import jax.numpy as jnp
from jax import lax
from jax.experimental import pallas as pl
from jax.experimental.pallas import tpu as pltpu

F32 = jnp.float32
BF16 = jnp.bfloat16

D_MODEL = 1024
CHUNK = 64
RMS_EPS = 1e-6
L2_EPS = 1e-6

S5_WIDTH = 512
S5_GROUP = 16
S5_GROUPS = 32
S5_P = 64
S5_STATE = S5_GROUPS * S5_P
S5_BLOCKS = 4
S5_BLOCK_STATE = S5_STATE // S5_BLOCKS

GDN_HEADS = 4
GDN_DK = 128
GDN_DV = 128
GDN_QK = GDN_HEADS * GDN_DK
GDN_V = GDN_HEADS * GDN_DV
GDN_CONV = 4
GDN_CONV_CH = 2 * GDN_QK + GDN_V
SOLVE_SPLIT_WIDTH = 8
GDN_ROWS_PER_STEP = 8
CONV_GROUP = 512
TAIL_ROWS = 8

SWA_HEADS = 16
SWA_KV_HEADS = 4
SWA_GROUPS = 4
SWA_HD = 64
KV_WIN = 128
KV_COLS = 2 * SWA_KV_HEADS * SWA_HD
SWA_ROWS_PER_STEP = 256
SWA_STAGE_CHUNKS = 2

D_FF = 2816
FF_CHUNK = 256

PROJ_QKV = 0
PROJ_U = GDN_CONV_CH
PROJ_Z = PROJ_U + S5_WIDTH
PROJ_BA = PROJ_Z + GDN_V
PROJ_COLS = PROJ_BA + 128

LANES = 128
VMEM_LIMIT = 56 * 1024 * 1024


def _rms(x, w):
    return x * lax.rsqrt(jnp.mean(x * x, axis=-1, keepdims=True) + RMS_EPS) * w


def _dot(a, b):
    return jnp.dot(a.astype(BF16), b.astype(BF16), preferred_element_type=F32)


def _dot_nt(a, b):
    return lax.dot_general(a.astype(BF16), b.astype(BF16), (((1,), (1,)), ((), ())),
                           preferred_element_type=F32)


def _const_spec(shape):
    return pl.BlockSpec(shape, lambda *_: (0,) * len(shape), pipeline_mode=pl.Buffered(1))


def _in_proj_kernel(x_ref, nw_ref, w_ref, cw_ref, buf_ref, o_ref, tail_ref, prev, *, tm, tiles_per_seq, tail_tile,
                    tail_off):
    it = pl.program_id(0) % tiles_per_seq
    pad = TAIL_ROWS

    @pl.when(it == 0)
    def _():
        prev[...] = jnp.zeros((pad, GDN_CONV_CH), F32)
        prev[pad - (GDN_CONV - 1):pad, :] = buf_ref[...]

    h = _rms(x_ref[...], nw_ref[...]).astype(BF16)
    row = lax.broadcasted_iota(jnp.int32, (pad, CONV_GROUP), 0)
    tails = []
    for c0 in range(0, GDN_CONV_CH, CONV_GROUP):
        cols = slice(c0, c0 + CONV_GROUP)
        x = jnp.dot(h, w_ref[:, cols], preferred_element_type=F32)
        p8 = prev[:, cols]
        y = x * cw_ref[GDN_CONV - 1:GDN_CONV, cols]
        for k in range(1, GDN_CONV):
            r = pltpu.roll(x, k, axis=0)
            head = jnp.where(row < k, pltpu.roll(p8, k, axis=0), r[:pad])
            y = y + jnp.concatenate([head, r[pad:]], axis=0) * cw_ref[GDN_CONV - 1 - k:GDN_CONV - k, cols]
        tails.append((x[tail_off:tail_off + pad], x[tm - pad:]))
        y = jax.nn.silu(y)
        for h0 in range(c0, c0 + CONV_GROUP, GDN_DK):
            x_h = y[:, h0 - c0:h0 - c0 + GDN_DK]
            if h0 < 2 * GDN_QK:
                scale = GDN_DK ** -0.5 if h0 < GDN_QK else 1.0
                x_h = x_h * (lax.rsqrt(jnp.sum(x_h * x_h, axis=-1, keepdims=True) + L2_EPS) * scale)
            o_ref[:, h0:h0 + GDN_DK] = x_h
    o_ref[:, GDN_CONV_CH:] = jnp.dot(h, w_ref[:, GDN_CONV_CH:], preferred_element_type=F32)
    prev[...] = jnp.concatenate([t[1] for t in tails], axis=1)

    @pl.when(it == tail_tile)
    def _():
        tail_ref[...] = jnp.concatenate([t[0] for t in tails], axis=1)


def _in_proj(x3d, n_valid, norm_w, w_perm, conv_w, conv_buf, tm):
    bsz, lp, _ = x3d.shape
    t = bsz * lp
    tiles_per_seq = lp // tm
    assert n_valid >= TAIL_ROWS and (n_valid - TAIL_ROWS) // tm == (n_valid - 1) // tm
    tail_tile = (n_valid - 1) // tm
    kern = functools.partial(_in_proj_kernel, tm=tm, tiles_per_seq=tiles_per_seq, tail_tile=tail_tile,
                             tail_off=n_valid - TAIL_ROWS - tail_tile * tm)
    return pl.pallas_call(
        kern,
        out_shape=(jax.ShapeDtypeStruct((t, PROJ_COLS), F32),
                   jax.ShapeDtypeStruct((bsz, TAIL_ROWS, GDN_CONV_CH), F32)),
        grid=(t // tm,),
        in_specs=[pl.BlockSpec((tm, D_MODEL), lambda i: (i, 0)),
                  _const_spec((1, D_MODEL)),
                  _const_spec((D_MODEL, PROJ_COLS)),
                  _const_spec((GDN_CONV, GDN_CONV_CH)),
                  pl.BlockSpec((None, GDN_CONV - 1, GDN_CONV_CH), lambda i: (i // tiles_per_seq, 0, 0))],
        out_specs=(pl.BlockSpec((tm, PROJ_COLS), lambda i: (i, 0)),
                   pl.BlockSpec((None, TAIL_ROWS, GDN_CONV_CH), lambda i: (i // tiles_per_seq, 0, 0))),
        scratch_shapes=[pltpu.VMEM((TAIL_ROWS, GDN_CONV_CH), F32)],
        compiler_params=pltpu.CompilerParams(dimension_semantics=("arbitrary",),
                                             vmem_limit_bytes=VMEM_LIMIT),
        name="in_proj",
    )(x3d.reshape(t, D_MODEL), norm_w, w_perm, conv_w, conv_buf)


def _s5_kernel(u_ref, x0r_ref, x0i_ref, lamr_ref, lami_ref, wb_ref, wc_ref, d_ref, wglu_ref, bglu_ref,
               o_ref, xr_out, xi_out, st_r, st_i, xs, *, tc, bsz):
    i = pl.program_id(0)

    @pl.when(i == 0)
    def _():
        st_r[...] = x0r_ref[...]
        st_i[...] = x0i_ref[...]

    u_tm = jnp.swapaxes(u_ref[...], 0, 1).reshape(tc * bsz, S5_WIDTH)
    half = S5_BLOCK_STATE
    ys = []
    for j in range(S5_BLOCKS):
        bu = _dot(u_tm[:, j * LANES:(j + 1) * LANES], wb_ref[j])
        xs[j] = bu.reshape(tc, bsz, 2 * half)
        lr = jnp.broadcast_to(lamr_ref[:, j * half:(j + 1) * half], (bsz, half))
        li = jnp.broadcast_to(lami_ref[:, j * half:(j + 1) * half], (bsz, half))
        xr = st_r[:, j * half:(j + 1) * half]
        xi = st_i[:, j * half:(j + 1) * half]
        for t in range(tc):
            xr, xi = (lr * xr - li * xi + xs[j, t, :, :half], lr * xi + li * xr + xs[j, t, :, half:])
            xs[j, t, :, :half] = xr
            xs[j, t, :, half:] = xi
        st_r[:, j * half:(j + 1) * half] = xr
        st_i[:, j * half:(j + 1) * half] = xi
        ys.append(_dot(xs[j].reshape(tc * bsz, 2 * half), wc_ref[j]))
    y = jnp.concatenate(ys, axis=-1) + d_ref[...] * u_tm
    z = jax.nn.gelu(y)
    out = z * jax.nn.sigmoid(_dot(z, wglu_ref[...]) + bglu_ref[...])
    o_ref[...] = jnp.swapaxes(out.reshape(tc, bsz, S5_WIDTH), 0, 1)
    xr_out[...] = st_r[...]
    xi_out[...] = st_i[...]


def _s5(proj3, n_valid, x0r, x0i, lamr, lami, wb, wc, d_row, wglu, bglu_row, tc):
    bsz = proj3.shape[0]
    kern = functools.partial(_s5_kernel, tc=tc, bsz=bsz)
    return pl.pallas_call(
        kern,
        out_shape=(jax.ShapeDtypeStruct((bsz, n_valid, S5_WIDTH), F32),
                   jax.ShapeDtypeStruct((bsz, S5_STATE), F32),
                   jax.ShapeDtypeStruct((bsz, S5_STATE), F32)),
        grid=(n_valid // tc,),
        in_specs=[pl.BlockSpec((bsz, tc, S5_WIDTH), lambda i: (0, i, PROJ_U // S5_WIDTH)),
                  _const_spec((bsz, S5_STATE)), _const_spec((bsz, S5_STATE)),
                  _const_spec((1, S5_STATE)), _const_spec((1, S5_STATE)),
                  _const_spec((S5_BLOCKS, LANES, 2 * S5_BLOCK_STATE)),
                  _const_spec((S5_BLOCKS, 2 * S5_BLOCK_STATE, LANES)),
                  _const_spec((1, S5_WIDTH)), _const_spec((S5_WIDTH, S5_WIDTH)), _const_spec((1, S5_WIDTH))],
        out_specs=(pl.BlockSpec((bsz, tc, S5_WIDTH), lambda i: (0, i, 0)),
                   pl.BlockSpec((bsz, S5_STATE), lambda i: (0, 0)),
                   pl.BlockSpec((bsz, S5_STATE), lambda i: (0, 0))),
        scratch_shapes=[pltpu.VMEM((bsz, S5_STATE), F32), pltpu.VMEM((bsz, S5_STATE), F32),
                        pltpu.VMEM((S5_BLOCKS, tc, bsz, 2 * S5_BLOCK_STATE), F32)],
        compiler_params=pltpu.CompilerParams(dimension_semantics=("arbitrary",),
                                             vmem_limit_bytes=VMEM_LIMIT),
        name="s5_mixer",
    )(proj3, x0r, x0i, lamr, lami, wb, wc, d_row, wglu, bglu_row)


def _solve_unit_lower(a_strict, rhs, block):
    n = len(rhs)
    w = rhs[0].shape[1]

    def hi_lo(x):
        hi = x.astype(BF16)
        return [hi, (x - hi.astype(F32)).astype(BF16)]

    m = [(-a).astype(BF16) for a in a_strict]
    x = list(rhs)
    width = 1
    while width < block:
        last = 2 * width >= block
        if width < SOLVE_SPLIT_WIDTH:
            p = [jnp.dot(m[j], jnp.concatenate(hi_lo(x[j]) + ([] if last else [m[j]]), axis=1),
                         preferred_element_type=F32) for j in range(n)]
            x = [x[j] + (p[j][:, :w] + p[j][:, w:2 * w]) for j in range(n)]
            if not last:
                m = [p[j][:, 2 * w:].astype(BF16) for j in range(n)]
        else:
            x = [x[j] + jnp.dot(m[j], x[j].astype(BF16), preferred_element_type=F32) for j in range(n)]
            if not last:
                m = [jnp.dot(m[j], m[j], preferred_element_type=F32).astype(BF16) for j in range(n)]
        width *= 2
    return x


def _gdn_kernel(qkv_ref, z_ref, ba_ref, s0_ref, alog_ref, dtb_ref, nw_ref, o_ref, s_out, s_st, *, tt, n_valid,
                rows_per_step):
    i = pl.program_id(1)
    nc = tt // CHUNK

    @pl.when(i == 0)
    def _():
        s_st[...] = s0_ref[...]

    ri = lax.broadcasted_iota(jnp.int32, (tt, tt), 0)
    ci = lax.broadcasted_iota(jnp.int32, (tt, tt), 1)
    same = (ri // CHUNK) == (ci // CHUNK)
    incl = jnp.logical_and(same, ri >= ci)
    strict = jnp.logical_and(same, ri > ci)
    row = lax.broadcasted_iota(jnp.int32, (tt, LANES), 0)
    pos = row % CHUNK

    units = [(r, h) for r in range(rows_per_step) for h in range(GDN_HEADS)]
    gates = []
    for r in range(rows_per_step):
        ba = ba_ref[r]
        beta_all = jax.nn.sigmoid(ba)
        g_all = -jnp.exp(alog_ref[...]) * jax.nn.softplus(ba + dtb_ref[...])
        if n_valid % tt != 0:
            live = (i * tt + row) < n_valid
            beta_all = jnp.where(live, beta_all, 0.0)
            g_all = jnp.where(live, g_all, 0.0)
        gc_all = g_all
        shift = 1
        while shift < CHUNK:
            gc_all = gc_all + jnp.where(pos >= shift, pltpu.roll(gc_all, shift, axis=0), 0.0)
            shift *= 2
        g_end_all = jnp.concatenate(
            [jnp.broadcast_to(gc_all[c * CHUNK + CHUNK - 1:(c + 1) * CHUNK, :], (CHUNK, LANES)) for c in range(nc)],
            axis=0)
        gates.append((beta_all, gc_all, g_end_all, gc_all.T, (g_end_all - gc_all).T))

    kq, kb, kt, vb, qn = [], [], [], [], []
    for r, h in units:
        q_h = qkv_ref[r, :, h * GDN_DK:(h + 1) * GDN_DK]
        k_h = qkv_ref[r, :, GDN_QK + h * GDN_DK:GDN_QK + (h + 1) * GDN_DK]
        v_h = qkv_ref[r, :, 2 * GDN_QK + h * GDN_DV:2 * GDN_QK + (h + 1) * GDN_DV]
        beta_b = jnp.broadcast_to(gates[r][0][:, h:h + 1], (tt, GDN_DK))
        kb.append(k_h * beta_b)
        vb.append(v_h * beta_b)
        kt.append(k_h.T)
        qn.append(q_h)
        kq.append(_dot(jnp.concatenate([kb[-1], q_h], axis=0), kt[-1]))
    a_mat, attn, rhs, q_dec, kdt = [], [], [], [], []
    for n, (r, h) in enumerate(units):
        _, gc_all, _, gc_t, rest_t = gates[r]
        gc_b = jnp.broadcast_to(gc_all[:, GDN_HEADS + h:GDN_HEADS + h + 1], (tt, max(tt, GDN_DK)))
        gexp_b = jnp.exp(gc_b[:, :GDN_DK])
        grow = gc_t[GDN_HEADS + h:GDN_HEADS + h + 1, :]
        decay = jnp.where(incl, jnp.exp(jnp.where(incl, gc_b[:, :tt] - grow, 0.0)), 0.0)
        a_mat.append(jnp.where(strict, kq[n][:tt] * decay, 0.0))
        attn.append(kq[n][tt:] * decay)
        rhs.append(jnp.concatenate([vb[n], kb[n] * gexp_b], axis=1))
        q_dec.append(qn[n] * gexp_b)
        kdt.append(kt[n] * jnp.exp(rest_t[GDN_HEADS + h:GDN_HEADS + h + 1, :]))
    uw = _solve_unit_lower(a_mat, rhs, CHUNK)
    u_val = [x[:, :GDN_DV] for x in uw]
    w_key = [x[:, GDN_DV:] for x in uw]

    s = [s_st[r, h] for r, h in units]
    o_state = [[] for _ in units]
    v_new = [[] for _ in units]
    for c in range(nc):
        rows = slice(c * CHUNK, (c + 1) * CHUNK)
        wq = [_dot(jnp.concatenate([w_key[n][rows], q_dec[n][rows]], axis=0), s[n]) for n in range(len(units))]
        for n in range(len(units)):
            o_state[n].append(wq[n][CHUNK:])
            v_new[n].append(u_val[n][rows] - wq[n][:CHUNK])
        upd = [_dot(kdt[n][:, rows], v_new[n][c]) for n in range(len(units))]
        for n, (r, h) in enumerate(units):
            g_last = jnp.exp(gates[r][2][c * CHUNK:c * CHUNK + 1, GDN_HEADS + h:GDN_HEADS + h + 1])
            s[n] = s[n] * g_last + upd[n]
    o_all = [jnp.concatenate(o_state[n], axis=0) + _dot(attn[n], jnp.concatenate(v_new[n], axis=0))
             for n in range(len(units))]
    for n, (r, h) in enumerate(units):
        s_st[r, h] = s[n]
        s_out[r, h] = s[n]
        o = _rms(o_all[n], nw_ref[...])
        o_ref[r, :, h * GDN_DV:(h + 1) * GDN_DV] = o * jax.nn.silu(z_ref[r, :, h * GDN_DV:(h + 1) * GDN_DV])


def _gdn(proj3, n_valid, s0, alog_row, dtb_row, nw_row, tt):
    bsz, lp, _ = proj3.shape
    rps = GDN_ROWS_PER_STEP
    kern = functools.partial(_gdn_kernel, tt=tt, n_valid=n_valid, rows_per_step=rps)
    return pl.pallas_call(
        kern,
        out_shape=(jax.ShapeDtypeStruct((bsz, lp, GDN_V), F32),
                   jax.ShapeDtypeStruct((bsz, GDN_HEADS, GDN_DK, GDN_DV), F32)),
        grid=(bsz // rps, lp // tt),
        in_specs=[pl.BlockSpec((rps, tt, GDN_CONV_CH), lambda b, i: (b, i, PROJ_QKV // GDN_CONV_CH)),
                  pl.BlockSpec((rps, tt, GDN_V), lambda b, i: (b, i, PROJ_Z // GDN_V)),
                  pl.BlockSpec((rps, tt, LANES), lambda b, i: (b, i, PROJ_BA // LANES)),
                  pl.BlockSpec((rps, GDN_HEADS, GDN_DK, GDN_DV), lambda b, i: (b, 0, 0, 0)),
                  _const_spec((1, LANES)), _const_spec((1, LANES)), _const_spec((1, GDN_DV))],
        out_specs=(pl.BlockSpec((rps, tt, GDN_V), lambda b, i: (b, i, 0)),
                   pl.BlockSpec((rps, GDN_HEADS, GDN_DK, GDN_DV), lambda b, i: (b, 0, 0, 0))),
        scratch_shapes=[pltpu.VMEM((rps, GDN_HEADS, GDN_DK, GDN_DV), F32)],
        compiler_params=pltpu.CompilerParams(dimension_semantics=("arbitrary", "arbitrary"),
                                             vmem_limit_bytes=VMEM_LIMIT),
        name="gdn_mixer",
    )(proj3, proj3, proj3, s0, alog_row, dtb_row, nw_row)


def _swiglu(x, nw_ref, wg_ref, wu_ref, wd_ref):
    h = _rms(x, nw_ref[...]).astype(BF16)
    acts = []
    for f in range(0, D_FF, FF_CHUNK):
        g = jnp.dot(h, wg_ref[:, f:f + FF_CHUNK], preferred_element_type=F32)
        u = jnp.dot(h, wu_ref[:, f:f + FF_CHUNK], preferred_element_type=F32)
        acts.append((jax.nn.silu(g) * u).astype(BF16))
    return x + jnp.dot(jnp.concatenate(acts, axis=1), wd_ref[...], preferred_element_type=F32)


def _ffn_specs():
    return [_const_spec((1, D_MODEL)), _const_spec((D_MODEL, D_FF)), _const_spec((D_MODEL, D_FF)),
            _const_spec((D_FF, D_MODEL))]


def _mix0_kernel(x_ref, a_ref, b_ref, wo_ref, nw_ref, wg_ref, wu_ref, wd_ref, nw1_ref, wkv_ref,
                 o_ref, kv_ref):
    mix = _dot(a_ref[...], wo_ref[:S5_WIDTH, :]) + _dot(b_ref[...], wo_ref[S5_WIDTH:, :])
    x2 = _swiglu(x_ref[...] + mix, nw_ref, wg_ref, wu_ref, wd_ref)
    o_ref[...] = x2
    kv_ref[...] = _dot(_rms(x2, nw1_ref[...]), wkv_ref[...])


def _mix0(x2d, a2d, b2d, wo, nw, wg, wu, wd, nw1, wkv, tm):
    t = x2d.shape[0]
    return pl.pallas_call(
        _mix0_kernel,
        out_shape=(jax.ShapeDtypeStruct((t, D_MODEL), F32), jax.ShapeDtypeStruct((t, KV_COLS), F32)),
        grid=(t // tm,),
        in_specs=[pl.BlockSpec((tm, D_MODEL), lambda i: (i, 0)),
                  pl.BlockSpec((tm, S5_WIDTH), lambda i: (i, 0)),
                  pl.BlockSpec((tm, GDN_V), lambda i: (i, 0)),
                  _const_spec((S5_WIDTH + GDN_V, D_MODEL))] + _ffn_specs() +
                 [_const_spec((1, D_MODEL)), _const_spec((D_MODEL, KV_COLS))],
        out_specs=(pl.BlockSpec((tm, D_MODEL), lambda i: (i, 0)),
                   pl.BlockSpec((tm, KV_COLS), lambda i: (i, 0))),
        compiler_params=pltpu.CompilerParams(dimension_semantics=("arbitrary",),
                                             vmem_limit_bytes=VMEM_LIMIT),
        name="mix0_ffn0",
    )(x2d, a2d, b2d, wo, nw, wg, wu, wd, nw1, wkv)


def _swa_kernel(sink_ref, x_ref, kv_ref, kvp_ref, nwm_ref, wq_ref, wo_ref, nw_ref, wg_ref, wu_ref, wd_ref,
                nwf_ref, o_ref, kvc, att, *, tq, cq, n_valid, has_cache, seqs):
    i = pl.program_id(1)
    x = x_ref[...].reshape(seqs * tq, D_MODEL)
    q = (_dot(_rms(x, nwm_ref[...]), wq_ref[...]) * (SWA_HD ** -0.5)).astype(BF16)
    win = KV_WIN + cq
    rows_all = KV_WIN + tq
    low = lax.broadcasted_iota(jnp.int32, (rows_all, LANES), 1) < SWA_HD
    col = lax.broadcasted_iota(jnp.int32, (2 * cq, win), 1)
    top = lax.broadcasted_iota(jnp.int32, (2 * cq, 1), 0) < cq
    n_chunks = 1 if has_cache else tq // cq
    if n_chunks * cq < tq:
        att[...] = jnp.zeros((seqs * tq, D_MODEL), F32)

    def soft(sc, sink):
        m = jnp.maximum(jnp.max(sc, axis=-1, keepdims=True), sink)
        p = jnp.exp(sc - m)
        return p.astype(BF16), 1.0 / (jnp.sum(p, axis=-1, keepdims=True) + jnp.exp(sink - m))

    for sq in range(seqs):
        kvc[sq, 0:KV_WIN, :] = kvp_ref[sq]
        kvc[sq, KV_WIN:KV_WIN + tq, :] = kv_ref[sq]
        k_lo, k_hi, v_lo, v_hi = [], [], [], []
        for dst_lo, dst_hi, base in ((k_lo, k_hi, 0), (v_lo, v_hi, SWA_KV_HEADS * SWA_HD)):
            for j in range(SWA_KV_HEADS // 2):
                t = kvc[sq, :, base + j * LANES:base + (j + 1) * LANES]
                r = pltpu.roll(t, SWA_HD, axis=1)
                dst_lo += [jnp.where(low, t, 0.0).astype(BF16), jnp.where(low, r, 0.0).astype(BF16)]
                dst_hi += [jnp.where(low, 0.0, r).astype(BF16), jnp.where(low, 0.0, t).astype(BF16)]
        for c0 in range(0, n_chunks, SWA_STAGE_CHUNKS):
            units = [(c, kh) for c in range(c0, min(c0 + SWA_STAGE_CHUNKS, n_chunks)) for kh in range(SWA_KV_HEADS)]
            scores = []
            for c, kh in units:
                r0 = sq * tq + c * cq
                qs = jnp.concatenate([q[r0:r0 + cq, (2 * kh) * LANES:(2 * kh + 1) * LANES],
                                      q[r0:r0 + cq, (2 * kh + 1) * LANES:(2 * kh + 2) * LANES]], axis=0)
                keys = slice(c * cq, c * cq + win)
                scores.append((_dot_nt(qs, k_lo[kh][keys]), _dot_nt(qs, k_hi[kh][keys])))
            probs = []
            for (c, kh), (s_e, s_o) in zip(units, scores):
                key_row = col + c * cq
                if has_cache:
                    ok = key_row < KV_WIN + n_valid
                elif c * cq < KV_WIN:
                    ok = jnp.logical_or(key_row >= KV_WIN, i > 0)
                else:
                    ok = None
                if ok is not None:
                    s_e, s_o = jnp.where(ok, s_e, -1e30), jnp.where(ok, s_o, -1e30)
                h0 = kh * SWA_GROUPS
                probs.append(soft(s_e, jnp.where(top, sink_ref[h0], sink_ref[h0 + 2]))
                             + soft(s_o, jnp.where(top, sink_ref[h0 + 1], sink_ref[h0 + 3])))
            for (c, kh), (p_e, inv_e, p_o, inv_o) in zip(units, probs):
                r0 = sq * tq + c * cq
                keys = slice(c * cq, c * cq + win)
                o = inv_e * jnp.dot(p_e, v_lo[kh][keys], preferred_element_type=F32) \
                    + inv_o * jnp.dot(p_o, v_hi[kh][keys], preferred_element_type=F32)
                att[r0:r0 + cq, (2 * kh) * LANES:(2 * kh + 1) * LANES] = o[:cq]
                att[r0:r0 + cq, (2 * kh + 1) * LANES:(2 * kh + 2) * LANES] = o[cq:]
    x3 = x + _dot(att[...], wo_ref[...])
    x4 = _swiglu(x3, nw_ref, wg_ref, wu_ref, wd_ref)
    o_ref[...] = _rms(x4, nwf_ref[...]).reshape(seqs, tq, D_MODEL)


def _swa(sinks, x3d, kv3d, kv_prev, n_valid, has_cache, nwm, wq, wo, nw, wg, wu, wd, nwf, tq, cq):
    bsz, lp, _ = x3d.shape
    seqs = max(1, min(bsz, SWA_ROWS_PER_STEP // tq)) if has_cache else 1
    assert bsz % seqs == 0
    kern = functools.partial(_swa_kernel, tq=tq, cq=cq, n_valid=n_valid, has_cache=has_cache, seqs=seqs)
    per = tq // KV_WIN
    if has_cache:
        prev_map = lambda b, i, s: (b, 0, 0)
    else:
        prev_map = lambda b, i, s: (b, jnp.maximum(i * per - 1, 0), 0)
    grid_spec = pltpu.PrefetchScalarGridSpec(
        num_scalar_prefetch=1,
        grid=(bsz // seqs, lp // tq),
        in_specs=[pl.BlockSpec((seqs, tq, D_MODEL), lambda b, i, s: (b, i, 0)),
                  pl.BlockSpec((seqs, tq, KV_COLS), lambda b, i, s: (b, i, 0)),
                  pl.BlockSpec((seqs, KV_WIN, KV_COLS), prev_map),
                  _const_spec((1, D_MODEL)), _const_spec((D_MODEL, D_MODEL)), _const_spec((D_MODEL, D_MODEL))]
                 + _ffn_specs() + [_const_spec((1, D_MODEL))],
        out_specs=pl.BlockSpec((seqs, tq, D_MODEL), lambda b, i, s: (b, i, 0)),
        scratch_shapes=[pltpu.VMEM((seqs, KV_WIN + tq, KV_COLS), F32), pltpu.VMEM((seqs * tq, D_MODEL), F32)],
    )
    return pl.pallas_call(
        kern,
        out_shape=jax.ShapeDtypeStruct((bsz, lp, D_MODEL), F32),
        grid_spec=grid_spec,
        compiler_params=pltpu.CompilerParams(dimension_semantics=("arbitrary", "arbitrary"),
                                             vmem_limit_bytes=VMEM_LIMIT),
        name="swa_ffn1",
    )(sinks, x3d, kv3d, kv_prev, nwm, wq, wo, nw, wg, wu, wd, nwf)


def _prep_params(p):
    q = {}
    w_in = p["w_in"][0]
    off_qkv = S5_WIDTH
    off_z = off_qkv + GDN_CONV_CH
    off_b = off_z + GDN_V
    w_perm = jnp.concatenate(
        [w_in[:, off_qkv:off_z], w_in[:, :S5_WIDTH], w_in[:, off_z:off_b], w_in[:, off_b:],
         jnp.zeros((D_MODEL, LANES - 2 * GDN_HEADS), F32)], axis=1)
    q["w_in"] = w_perm.astype(BF16)

    lam_re, lam_im = p["s5_lam_re"][0], p["s5_lam_im"][0]
    dt = jnp.exp(p["s5_log_dt"][0])[:, None]
    mag = jnp.exp(lam_re * dt)
    lbr, lbi = mag * jnp.cos(lam_im * dt), mag * jnp.sin(lam_im * dt)
    den = lam_re * lam_re + lam_im * lam_im
    nr, ni = lbr - 1.0, lbi
    cr = (nr * lam_re + ni * lam_im) / den
    ci = (ni * lam_re - nr * lam_im) / den
    b_re, b_im = p["s5_b_re"][0], p["s5_b_im"][0]
    bbr = cr[..., None] * b_re - ci[..., None] * b_im
    bbi = cr[..., None] * b_im + ci[..., None] * b_re
    gb = S5_GROUPS // S5_BLOCKS
    eye = jnp.eye(gb, dtype=F32)

    def in_blocks(m):
        m = m.reshape(S5_BLOCKS, gb, S5_P, S5_GROUP)
        return jnp.einsum("jgpc,gh->jgchp", m, eye).reshape(S5_BLOCKS, gb * S5_GROUP, gb * S5_P)

    def out_blocks(m):
        m = m.reshape(S5_BLOCKS, gb, S5_GROUP, S5_P)
        return jnp.einsum("jgcp,gh->jgphc", m, eye).reshape(S5_BLOCKS, gb * S5_P, gb * S5_GROUP)

    q["s5_wb"] = jnp.concatenate([in_blocks(bbr), in_blocks(bbi)], axis=2).astype(BF16)
    q["s5_wc"] = jnp.concatenate([out_blocks(p["s5_c_re"][0]), -out_blocks(p["s5_c_im"][0])], axis=1).astype(BF16)
    q["s5_lamr"] = lbr.reshape(1, S5_STATE)
    q["s5_lami"] = lbi.reshape(1, S5_STATE)
    q["s5_d"] = p["s5_d"][0].reshape(1, S5_WIDTH)
    q["s5_wglu"] = p["s5_w_glu"][0].astype(BF16)
    q["s5_bglu"] = p["s5_b_glu"][0].reshape(1, S5_WIDTH)

    def head_row(v):
        return jnp.zeros((1, LANES), F32).at[0, GDN_HEADS:2 * GDN_HEADS].set(v)

    q["gdn_alog"] = head_row(p["gdn_a_log"][0])
    q["gdn_dtb"] = head_row(p["gdn_dt_bias"][0])
    q["gdn_conv_w"] = p["gdn_conv_w"][0]
    q["gdn_nw"] = p["gdn_norm_w"][0].reshape(1, GDN_DV)
    q["w_out"] = p["w_out_ab"][0].astype(BF16)
    q["wq"] = p["swa_wq"][0].astype(BF16)
    q["wkv"] = jnp.concatenate([p["swa_wk"][0], p["swa_wv"][0]], axis=1).astype(BF16)
    q["wo"] = p["swa_wo"][0].astype(BF16)
    q["sinks"] = p["swa_sinks"][0]
    for name in ("norm_mix", "norm_ffn"):
        q[name] = p[name].reshape(-1, 1, D_MODEL)
    q["norm_final"] = p["norm_final"].reshape(1, D_MODEL)
    for name in ("ffn_w_gate", "ffn_w_up", "ffn_w_down"):
        q[name] = p[name].astype(BF16)
    return q


def _tile(n, target):
    t = min(n, target)
    while n % t:
        t //= 2
    return t


def _trunk(x, s5_re, s5_im, gdn_s, gdn_conv, cache_k, cache_v, q, n_valid, tiles):
    bsz, lp, _ = x.shape
    has_cache = cache_k is not None
    t = bsz * lp
    tm = _tile(t, tiles["tm"])
    x2d = x.reshape(t, D_MODEL)

    proj, qkv_tail = _in_proj(x, n_valid, q["norm_mix"][0], q["w_in"], q["gdn_conv_w"], gdn_conv, min(tm, lp))
    proj = proj.reshape(bsz, lp, PROJ_COLS)
    a_out, x_re, x_im = _s5(proj, n_valid, s5_re.reshape(bsz, S5_STATE), s5_im.reshape(bsz, S5_STATE),
                            q["s5_lamr"], q["s5_lami"], q["s5_wb"], q["s5_wc"], q["s5_d"], q["s5_wglu"],
                            q["s5_bglu"], _tile(n_valid, tiles["tc"]))
    if n_valid != lp:
        a_out = jnp.pad(a_out, ((0, 0), (0, lp - n_valid), (0, 0)))
    b_out, s_fin = _gdn(proj, n_valid, gdn_s, q["gdn_alog"], q["gdn_dtb"], q["gdn_nw"], _tile(lp, tiles["tt"]))
    new_buf = qkv_tail[:, -(GDN_CONV - 1):]

    x2, kv = _mix0(x2d, a_out.reshape(t, S5_WIDTH), b_out.reshape(t, GDN_V), q["w_out"], q["norm_ffn"][0],
                   q["ffn_w_gate"][0], q["ffn_w_up"][0], q["ffn_w_down"][0], q["norm_mix"][1], q["wkv"], tm)
    kv = kv.reshape(bsz, lp, KV_COLS)
    half = SWA_KV_HEADS * SWA_HD
    if has_cache:
        kv_prev = jnp.concatenate([cache_k.reshape(bsz, KV_WIN, half), cache_v.reshape(bsz, KV_WIN, half)], axis=-1)
        tq, cq = lp, n_valid
    else:
        assert n_valid == lp
        kv_prev = kv
        tq, cq = _tile(lp, tiles["tq"]), CHUNK
    y = _swa(q["sinks"], x2.reshape(bsz, lp, D_MODEL), kv, kv_prev, n_valid, has_cache, q["norm_mix"][1],
             q["wq"], q["wo"], q["norm_ffn"][1], q["ffn_w_gate"][1], q["ffn_w_up"][1], q["ffn_w_down"][1],
             q["norm_final"], tq, cq)

    kv_tail = kv[:, max(n_valid - KV_WIN, 0):n_valid]
    if has_cache:
        kv_tail = jnp.concatenate([kv_prev, kv_tail], axis=1)[:, -KV_WIN:]
    ck = kv_tail[:, :, :half].reshape(bsz, KV_WIN, SWA_KV_HEADS, SWA_HD)
    cv = kv_tail[:, :, half:].reshape(bsz, KV_WIN, SWA_KV_HEADS, SWA_HD)
    return (y if n_valid == lp else y[:, :n_valid], x_re.reshape(1, bsz, S5_GROUPS, S5_P), x_im.reshape(1, bsz, S5_GROUPS, S5_P),
            s_fin[None], new_buf[None], ck[None], cv[None])


PROMPT_TILES = dict(tm=512, tc=128, tt=128, tq=512)
SAMPLE_PAD = 64


def kernel(x_prompt, x_sample, state_s5_re, state_s5_im, state_gdn, state_gdn_conv, cache_swa_k, cache_swa_v,
           norm_mix, norm_ffn, norm_final, w_in, s5_lam_re, s5_lam_im, s5_log_dt, s5_b_re, s5_b_im,
           s5_c_re, s5_c_im, s5_d, s5_w_glu, s5_b_glu, gdn_conv_w, gdn_a_log, gdn_dt_bias, gdn_norm_w,
           w_out_ab, swa_wq, swa_wk, swa_wv, swa_sinks, swa_wo, ffn_w_gate, ffn_w_up, ffn_w_down):
    p = dict(norm_mix=norm_mix, norm_ffn=norm_ffn, norm_final=norm_final, w_in=w_in,
             s5_lam_re=s5_lam_re, s5_lam_im=s5_lam_im, s5_log_dt=s5_log_dt, s5_b_re=s5_b_re, s5_b_im=s5_b_im,
             s5_c_re=s5_c_re, s5_c_im=s5_c_im, s5_d=s5_d, s5_w_glu=s5_w_glu, s5_b_glu=s5_b_glu,
             gdn_conv_w=gdn_conv_w, gdn_a_log=gdn_a_log, gdn_dt_bias=gdn_dt_bias, gdn_norm_w=gdn_norm_w,
             w_out_ab=w_out_ab, swa_wq=swa_wq, swa_wk=swa_wk, swa_wv=swa_wv, swa_sinks=swa_sinks,
             swa_wo=swa_wo, ffn_w_gate=ffn_w_gate, ffn_w_up=ffn_w_up, ffn_w_down=ffn_w_down)
    q = _prep_params(p)
    bsz, seq, _ = x_prompt.shape
    dbsz, dseq, _ = x_sample.shape
    zeros_s5 = jnp.zeros((bsz, S5_GROUPS, S5_P), F32)
    outs_p = _trunk(x_prompt, zeros_s5, zeros_s5, jnp.zeros((bsz, GDN_HEADS, GDN_DK, GDN_DV), F32),
                    jnp.zeros((bsz, GDN_CONV - 1, GDN_CONV_CH), F32), None, None, q, seq, PROMPT_TILES)
    x_s = jnp.pad(x_sample, ((0, 0), (0, SAMPLE_PAD - dseq), (0, 0)))
    outs_s = _trunk(x_s, state_s5_re[0], state_s5_im[0], state_gdn[0], state_gdn_conv[0],
                    cache_swa_k[0], cache_swa_v[0], q, dseq, PROMPT_TILES)
    return (outs_p[0], outs_s[0]) + tuple(outs_p[1:]) + tuple(outs_s[1:])
```
